```python
import jax, jax.numpy as jnp
from jax import lax
import numpy as np

D_MODEL = 1024
BATCH = 4
SEQ = 8192
DEPTH = 2

CTX_LEN = 256
GRID_W = 64

FOURIER_WIDTH = D_MODEL // 4
N_FOURIER_HEADS = 4
FOURIER_HEAD_DIM = FOURIER_WIDTH // N_FOURIER_HEADS

V_HEAD_DIM = 64
QK_NOPE_DIM = 64
QK_ROPE_DIM = 32
QK_HEAD_DIM = QK_NOPE_DIM + QK_ROPE_DIM
N_MLA_HEADS = (D_MODEL - FOURIER_WIDTH) // V_HEAD_DIM
MLA_WIDTH = N_MLA_HEADS * V_HEAD_DIM
Q_LORA_RANK = 384
KV_LORA_RANK = 128

MIX_WIDTH = FOURIER_WIDTH + MLA_WIDTH
IN_PROJ_WIDTH = FOURIER_WIDTH + Q_LORA_RANK + KV_LORA_RANK + QK_ROPE_DIM

D_FF = 2816
CONV_WIDTH = 3

ROPE_THETA = 10000.0
NORM_EPS = 1e-6
Q_BLOCK = 128
SOFTMAX_SCALE = QK_HEAD_DIM ** -0.5
N_MOD = 6

kernel_name = "hybrid_fourier_mla_convffn_dit"


def rms_norm(x, g):
    xf = x.astype(jnp.float32)
    y = xf * lax.rsqrt(jnp.mean(xf * xf, axis=-1, keepdims=True) + NORM_EPS)
    return (y * g.astype(jnp.float32)).astype(x.dtype)


def modulate(h, shift, scale):
    return h * (1 + scale) + shift


def axial_rope_tables(rows, dtype):
    row_ids = jnp.broadcast_to(jnp.arange(rows)[:, None], (rows, GRID_W)).reshape(-1).astype(jnp.float32)
    col_ids = jnp.broadcast_to(jnp.arange(GRID_W)[None, :], (rows, GRID_W)).reshape(-1).astype(jnp.float32)
    n_freq = QK_ROPE_DIM // 4
    inv_freq = ROPE_THETA ** (-jnp.arange(n_freq, dtype=jnp.float32) / n_freq)
    ang = jnp.concatenate([row_ids[:, None] * inv_freq, col_ids[:, None] * inv_freq], axis=-1)
    return jnp.cos(ang).astype(dtype), jnp.sin(ang).astype(dtype)


def apply_rope(x, cos, sin):
    x1, x2 = jnp.split(x, 2, axis=-1)
    return jnp.concatenate([x1 * cos - x2 * sin, x2 * cos + x1 * sin], axis=-1)


def fourier_mix(u, w_f, b_f):
    b, n, _ = u.shape
    uh = u.reshape(b, n, N_FOURIER_HEADS, FOURIER_HEAD_DIM).astype(jnp.float32)
    f = jnp.fft.fft2(uh, axes=(1, 3), norm="ortho").real
    f = f.reshape(b, n, FOURIER_WIDTH).astype(u.dtype)
    return f @ w_f + b_f


def mla_q(c_q, g_q, w_q_b, cos, sin):
    b, n, _ = c_q.shape
    q = (rms_norm(c_q, g_q) @ w_q_b).reshape(b, n, N_MLA_HEADS, QK_HEAD_DIM)
    q_nope, q_rope = q[..., :QK_NOPE_DIM], q[..., QK_NOPE_DIM:]
    if cos is not None:
        q_rope = apply_rope(q_rope, cos[:, None, :], sin[:, None, :])
    return q_nope, q_rope


def mla_kv(c_kv, k_rope, g_kv, w_kv_b, cos, sin):
    b, n, _ = c_kv.shape
    kv = (rms_norm(c_kv, g_kv) @ w_kv_b).reshape(b, n, N_MLA_HEADS, QK_NOPE_DIM + V_HEAD_DIM)
    k_nope, v = kv[..., :QK_NOPE_DIM], kv[..., QK_NOPE_DIM:]
    if cos is not None:
        k_rope = apply_rope(k_rope, cos, sin)
    return k_nope, k_rope, v


def mla_attend(q_nope, q_rope, k_nope, k_rope, v):
    s = jnp.einsum('bqhd,bkhd->bhqk', q_nope, k_nope) + jnp.einsum('bqhr,bkr->bhqk', q_rope, k_rope)
    p = jax.nn.softmax(s.astype(jnp.float32) * SOFTMAX_SCALE, axis=-1).astype(v.dtype)
    return jnp.einsum('bhqk,bkhd->bqhd', p, v)


def blocked_attention(q_nope, q_rope, k_nope, k_rope, v):
    b, n, h, _ = q_nope.shape
    nb = n // Q_BLOCK
    qn = q_nope.reshape(b, nb, Q_BLOCK, h, QK_NOPE_DIM).transpose(1, 0, 2, 3, 4)
    qr = q_rope.reshape(b, nb, Q_BLOCK, h, QK_ROPE_DIM).transpose(1, 0, 2, 3, 4)
    out = lax.map(lambda blk: mla_attend(blk[0], blk[1], k_nope, k_rope, v), (qn, qr))
    return out.transpose(1, 0, 2, 3, 4).reshape(b, n, h * V_HEAD_DIM)


def split_in_proj(p):
    f_in = p[..., :FOURIER_WIDTH]
    c_q = p[..., FOURIER_WIDTH:FOURIER_WIDTH + Q_LORA_RANK]
    c_kv = p[..., FOURIER_WIDTH + Q_LORA_RANK:FOURIER_WIDTH + Q_LORA_RANK + KV_LORA_RANK]
    k_rope = p[..., FOURIER_WIDTH + Q_LORA_RANK + KV_LORA_RANK:]
    return f_in, c_q, c_kv, k_rope


def conv_ffn(h, w_up, w_dw, b_dw, w_down):
    u = h @ w_up
    u = lax.conv_general_dilated(u, w_dw[:, None, :], window_strides=(1,),
                                 padding=((CONV_WIDTH // 2, CONV_WIDTH // 2),),
                                 dimension_numbers=('NWC', 'WIO', 'NWC'),
                                 feature_group_count=2 * D_FF) + b_dw
    gate, val = jnp.split(u, 2, axis=-1)
    return (jax.nn.silu(gate) * val) @ w_down


def setup_inputs(seed: int = 0) -> dict:
    key = jax.random.key(seed)
    ks = jax.random.split(key, 24)
    f32 = jnp.float32

    def nrm(k, shape, scale):
        return jax.random.normal(k, shape, f32) * scale

    def gain(k, shape):
        return 1.0 + 0.02 * jax.random.normal(k, shape, f32)

    L, D = DEPTH, D_MODEL
    return {
        "x": jax.random.normal(ks[0], (BATCH, SEQ, D), f32),
        "c": jax.random.normal(ks[1], (BATCH, D), f32),
        "ctx": jax.random.normal(ks[2], (BATCH, CTX_LEN, D), f32),
        "c_ctx": jax.random.normal(ks[3], (D,), f32),
        "w_ada": nrm(ks[4], (L, D, N_MOD * D), 0.5 * D ** -0.5),
        "b_ada": nrm(ks[5], (L, N_MOD * D), 0.02),
        "g_mix": gain(ks[6], (L, D)),
        "w_in": nrm(ks[7], (L, D, IN_PROJ_WIDTH), D ** -0.5),
        "w_fourier": nrm(ks[8], (L, FOURIER_WIDTH, FOURIER_WIDTH), FOURIER_WIDTH ** -0.5),
        "b_fourier": nrm(ks[9], (L, FOURIER_WIDTH), 0.02),
        "g_q_a": gain(ks[10], (L, Q_LORA_RANK)),
        "w_q_b": nrm(ks[11], (L, Q_LORA_RANK, N_MLA_HEADS * QK_HEAD_DIM), Q_LORA_RANK ** -0.5),
        "g_kv_a": gain(ks[12], (L, KV_LORA_RANK)),
        "w_kv_b": nrm(ks[13], (L, KV_LORA_RANK, N_MLA_HEADS * (QK_NOPE_DIM + V_HEAD_DIM)), KV_LORA_RANK ** -0.5),
        "w_out": nrm(ks[14], (L, MIX_WIDTH, D), MIX_WIDTH ** -0.5),
        "g_ffn": gain(ks[15], (L, D)),
        "w_up": nrm(ks[16], (L, D, 2 * D_FF), D ** -0.5),
        "w_dw": nrm(ks[17], (L, CONV_WIDTH, 2 * D_FF), CONV_WIDTH ** -0.5),
        "b_dw": nrm(ks[18], (L, 2 * D_FF), 0.02),
        "w_down": nrm(ks[19], (L, D_FF, D), D_FF ** -0.5),
        "g_final": gain(ks[20], (D,)),
    }


def reference(x, c, ctx, c_ctx, w_ada, b_ada, g_mix, w_in, w_fourier, b_fourier,
              g_q_a, w_q_b, g_kv_a, w_kv_b, w_out, g_ffn, w_up, w_dw, b_dw, w_down, g_final):
    b, n_lat, _ = x.shape
    ROWS = n_lat // GRID_W
    cos, sin = axial_rope_tables(ROWS, x.dtype)
    silu_c = jax.nn.silu(c)
    silu_cc = jax.nn.silu(c_ctx)

    for l in range(DEPTH):
        last = l == DEPTH - 1
        mod_lat = (silu_c @ w_ada[l] + b_ada[l])[:, None, :]
        mod_ctx = silu_cc @ w_ada[l] + b_ada[l]
        sh1, sc1, gt1, sh2, sc2, gt2 = jnp.split(mod_lat, N_MOD, axis=-1)
        csh1, csc1, cgt1, csh2, csc2, cgt2 = jnp.split(mod_ctx, N_MOD, axis=-1)

        h = modulate(rms_norm(x, g_mix[l]), sh1, sc1)
        hc = modulate(rms_norm(ctx, g_mix[l]), csh1, csc1)
        f_in, c_q, c_kv, k_rope_raw = split_in_proj(h @ w_in[l])
        f_in_c, c_q_c, c_kv_c, k_rope_raw_c = split_in_proj(hc @ w_in[l])

        kn_c, kr_c, v_c = mla_kv(c_kv_c, k_rope_raw_c, g_kv_a[l], w_kv_b[l], None, None)
        kn_l, kr_l, v_l = mla_kv(c_kv, k_rope_raw, g_kv_a[l], w_kv_b[l], cos, sin)
        qn_l, qr_l = mla_q(c_q, g_q_a[l], w_q_b[l], cos, sin)

        k_nope_all = jnp.concatenate([kn_l, kn_c], axis=1)
        k_rope_all = jnp.concatenate([kr_l, kr_c], axis=1)
        v_all = jnp.concatenate([v_l, v_c], axis=1)
        att_l = blocked_attention(qn_l, qr_l, k_nope_all, k_rope_all, v_all)
        four_l = fourier_mix(f_in, w_fourier[l], b_fourier[l])
        mix_l = jnp.concatenate([four_l, att_l], axis=-1) @ w_out[l]
        x = x + gt1 * mix_l

        x = x + gt2 * conv_ffn(modulate(rms_norm(x, g_ffn[l]), sh2, sc2),
                               w_up[l], w_dw[l], b_dw[l], w_down[l])

        if not last:
            qn_c, qr_c = mla_q(c_q_c, g_q_a[l], w_q_b[l], None, None)
            att_c = mla_attend(qn_c, qr_c, kn_c, kr_c, v_c).reshape(b, -1, MLA_WIDTH)
            four_c = fourier_mix(f_in_c, w_fourier[l], b_fourier[l])
            mix_c = jnp.concatenate([four_c, att_c], axis=-1) @ w_out[l]
            ctx = ctx + cgt1 * mix_c
            ctx = ctx + cgt2 * conv_ffn(modulate(rms_norm(ctx, g_ffn[l]), csh2, csc2),
                                        w_up[l], w_dw[l], b_dw[l], w_down[l])

    return rms_norm(x, g_final)
```

```python
import functools
import math

import numpy as np
import jax
import jax.numpy as jnp
from jax import lax
from jax.experimental import pallas as pl
from jax.experimental.pallas import tpu as pltpu

F32 = jnp.float32
BF16 = jnp.bfloat16

D_MODEL = 1024
CTX_LEN = 256
GRID_W = 64
FOURIER_WIDTH = 256
N_FOURIER_HEADS = 4
FOURIER_HEAD_DIM = 64
V_HEAD_DIM = 64
QK_NOPE_DIM = 64
QK_ROPE_DIM = 32
QK_HEAD_DIM = 96
N_HEADS = 12
MLA_WIDTH = 768
Q_LORA_RANK = 384
KV_LORA_RANK = 128
D_FF = 2816
ROPE_THETA = 10000.0
NORM_EPS = 1e-6
SOFTMAX_SCALE = QK_HEAD_DIM ** -0.5
N_MOD = 6

LANES = 128
BF16_SUBLANES = 16
HEAD_PAD = LANES
ROW_TILE = 256
KV_CHUNK = 512
FF_CHUNK = 256
DFT_N2 = 64
DFT_K1_TILE = 8
VMEM_LIMIT = 56 * 1024 * 1024
LOG2E = math.log2(math.e)
HIGHEST = lax.Precision.HIGHEST


def _cparams(*sem):
    return pltpu.CompilerParams(dimension_semantics=sem, vmem_limit_bytes=VMEM_LIMIT)


def _rms(x, eps=NORM_EPS):
    return x * lax.rsqrt(jnp.mean(x * x, axis=-1, keepdims=True) + eps)


def _mod_kernel(c_ref, w_ref, b_ref, o_ref):
    c = c_ref[...]
    s = c / (1.0 + jnp.exp(-c))
    o_ref[...] = jnp.dot(s.astype(BF16), w_ref[...].astype(BF16),
                         preferred_element_type=F32) + b_ref[...]


def _modulation(cvec, w_ada, b_ada):
    depth, d, _ = w_ada.shape
    rows = cvec.shape[0]
    return pl.pallas_call(
        _mod_kernel,
        grid=(depth, N_MOD),
        in_specs=[
            pl.BlockSpec((rows, d), lambda l, j: (0, 0)),
            pl.BlockSpec((None, d, d), lambda l, j: (l, 0, j)),
            pl.BlockSpec((None, 1, d), lambda l, j: (l, 0, j)),
        ],
        out_specs=pl.BlockSpec((None, rows, d), lambda l, j: (l, 0, j)),
        out_shape=jax.ShapeDtypeStruct((depth, rows, N_MOD * d), F32),
        compiler_params=_cparams("arbitrary", "arbitrary"),
        name="adaln_mod",
    )(cvec, w_ada, b_ada.reshape(depth, 1, N_MOD * d))


def _in_proj_kernel(x_ref, mod_ref, g_ref, w_in_ref, gq_ref, wq_ref, gkv_ref, wkv_ref,
                    cq_ref, sq_ref, ck_ref, sk_ref,
                    f_ref, q_ref, k_ref, v_ref):
    x = x_ref[...]
    gain = g_ref[...] * (1.0 + mod_ref[1:2, :])
    h = _rms(x) * gain + mod_ref[0:1, :]
    p = jnp.dot(h.astype(BF16), w_in_ref[...], preferred_element_type=F32)
    f_ref[...] = p[:, :FOURIER_WIDTH]

    o_q = FOURIER_WIDTH
    o_kv = o_q + Q_LORA_RANK
    o_kr = o_kv + KV_LORA_RANK
    hw = N_HEADS * HEAD_PAD

    cq = _rms(p[:, o_q:o_kv]) * gq_ref[...]
    qq = jnp.dot(cq.astype(BF16), wq_ref[...], preferred_element_type=F32)
    cos_q = cq_ref[...]
    sin_q = sq_ref[...]
    for h_i in range(N_HEADS):
        lo = h_i * HEAD_PAD
        q_ref[h_i] = (qq[:, lo:lo + HEAD_PAD] * cos_q
                      + qq[:, hw + lo:hw + lo + HEAD_PAD] * sin_q).astype(BF16)

    ckv = _rms(p[:, o_kv:o_kr]) * gkv_ref[...]
    kv = jnp.dot(ckv.astype(BF16), wkv_ref[...], preferred_element_type=F32)
    k_rope = p[:, o_kr:o_kr + HEAD_PAD] * ck_ref[...] + p[:, o_kr + HEAD_PAD:] * sk_ref[...]
    lane = lax.broadcasted_iota(jnp.int32, (1, HEAD_PAD), 1)
    ones_lane = (lane == V_HEAD_DIM).astype(F32)
    for h_i in range(N_HEADS):
        lo = h_i * HEAD_PAD
        k_ref[h_i] = (kv[:, lo:lo + HEAD_PAD] + k_rope).astype(BF16)
        v_ref[h_i] = (kv[:, hw + lo:hw + lo + HEAD_PAD] + ones_lane).astype(BF16)


def _in_proj(xx, mod_l, g_mix, w_in_p, g_q, w_q_p, g_kv, w_kv_p, tabs, n_lat):
    b, nt, d = xx.shape
    tiles = nt // ROW_TILE
    ctx_tile = n_lat // ROW_TILE
    cos_q, sin_q, cos_k, sin_k = tabs
    const = lambda bi, t: (0, 0)
    tab_spec = pl.BlockSpec((ROW_TILE, HEAD_PAD), lambda bi, t: (t, 0))
    head_spec = pl.BlockSpec((None, N_HEADS, ROW_TILE, HEAD_PAD), lambda bi, t: (bi, 0, t, 0))
    head_shape = jax.ShapeDtypeStruct((b, N_HEADS, nt, HEAD_PAD), BF16)
    return pl.pallas_call(
        _in_proj_kernel,
        grid=(b, tiles),
        in_specs=[
            pl.BlockSpec((None, ROW_TILE, d), lambda bi, t: (bi, t, 0)),
            pl.BlockSpec((None, N_MOD, d), lambda bi, t: (jnp.where(t == ctx_tile, b, bi), 0, 0)),
            pl.BlockSpec((1, d), const),
            pl.BlockSpec(w_in_p.shape, const),
            pl.BlockSpec((1, Q_LORA_RANK), const),
            pl.BlockSpec(w_q_p.shape, const),
            pl.BlockSpec((1, KV_LORA_RANK), const),
            pl.BlockSpec(w_kv_p.shape, const),
            tab_spec, tab_spec, tab_spec, tab_spec,
        ],
        out_specs=[
            pl.BlockSpec((None, ROW_TILE, FOURIER_WIDTH), lambda bi, t: (bi, t, 0)),
            head_spec, head_spec, head_spec,
        ],
        out_shape=[
            jax.ShapeDtypeStruct((b, nt, FOURIER_WIDTH), F32),
            head_shape, head_shape, head_shape,
        ],
        compiler_params=_cparams("arbitrary", "arbitrary"),
        name="in_proj",
    )(xx, mod_l, g_mix, w_in_p, g_q, w_q_p, g_kv, w_kv_p, cos_q, sin_q, cos_k, sin_k)


def _attention_kernel(q_ref, k_ref, v_ref, o_ref, *, n_lat):
    t = pl.program_id(2)
    is_lat = t < n_lat // ROW_TILE
    n_main = jnp.where(is_lat, n_lat // KV_CHUNK, 0)
    nt_dims = (((1,), (1,)), ((), ()))
    outs = []
    for hh in range(2):
        q = q_ref[hh]

        def step(kc, vc, m, acc, q=q):
            s = lax.dot_general(q, kc, nt_dims, preferred_element_type=F32)
            m_new = jnp.maximum(m, jnp.max(s, axis=-1, keepdims=True))
            p = jnp.exp2(s - m_new)
            alpha = jnp.exp2(m - m_new)
            acc = alpha * acc + jnp.dot(p.astype(BF16), vc, preferred_element_type=F32)
            return m_new, acc

        def body(c, carry, hh=hh, step=step):
            m, acc = carry
            start = pl.multiple_of(c * KV_CHUNK, KV_CHUNK)
            return step(k_ref[hh, pl.ds(start, KV_CHUNK), :], v_ref[hh, pl.ds(start, KV_CHUNK), :], m, acc)

        m0 = jnp.full((ROW_TILE, 1), -jnp.inf, F32)
        acc0 = jnp.zeros((ROW_TILE, HEAD_PAD), F32)
        m, acc = lax.fori_loop(0, n_main, body, (m0, acc0))
        m, acc = step(k_ref[hh, n_lat:, :], v_ref[hh, n_lat:, :], m, acc)
        outs.append(acc[:, :V_HEAD_DIM] / acc[:, V_HEAD_DIM:V_HEAD_DIM + 1])
    o_ref[...] = jnp.concatenate(outs, axis=-1).astype(o_ref.dtype)


def _attention(q, k, v, n_lat, q_tiles):
    b, _, nt, _ = q.shape
    kv_spec = pl.BlockSpec((None, 2, nt, HEAD_PAD), lambda bi, hp, t: (bi, hp, 0, 0))
    return pl.pallas_call(
        functools.partial(_attention_kernel, n_lat=n_lat),
        grid=(b, N_HEADS // 2, q_tiles),
        in_specs=[
            pl.BlockSpec((None, 2, ROW_TILE, HEAD_PAD), lambda bi, hp, t: (bi, hp, t, 0)),
            kv_spec, kv_spec,
        ],
        out_specs=pl.BlockSpec((None, ROW_TILE, 2 * V_HEAD_DIM), lambda bi, hp, t: (bi, t, hp)),
        out_shape=jax.ShapeDtypeStruct((b, q_tiles * ROW_TILE, MLA_WIDTH), BF16),
        compiler_params=_cparams("arbitrary", "arbitrary", "arbitrary"),
        name="mla_attention",
    )(q, k, v)


def _dft_mats(n):
    idx = np.arange(n)
    ang = 2.0 * np.pi * ((idx[:, None] * idx[None, :]) % n) / n
    return np.cos(ang), np.sin(ang)


def _channel_dft_mats():
    c, s = _dft_mats(FOURIER_HEAD_DIM)
    eye = np.eye(N_FOURIER_HEADS)
    return np.kron(eye, c), np.kron(eye, s)


def _dft_stage1_kernel(f1_ref, x_ref, y_ref):
    y_ref[...] = jnp.dot(f1_ref[...], x_ref[...], precision=HIGHEST, preferred_element_type=F32)


def _dft_stage2_kernel(yr_ref, yi_ref, tc_ref, ts_ref, f2_ref, cb_ref, sb_ref, wf_ref, bf_ref, o_ref):
    xr, xi = [], []
    for j in range(DFT_K1_TILE):
        yr, yi = yr_ref[j], yi_ref[j]
        tc = jnp.concatenate([tc_ref[j]] * (FOURIER_WIDTH // LANES), axis=-1)
        ts = jnp.concatenate([ts_ref[j]] * (FOURIER_WIDTH // LANES), axis=-1)
        z = jnp.concatenate([yr * tc + yi * ts, yi * tc - yr * ts], axis=0)
        xx = jnp.dot(f2_ref[...], z, precision=HIGHEST, preferred_element_type=F32)
        xr.append(xx[:DFT_N2])
        xi.append(xx[DFT_N2:])
    xr = jnp.concatenate(xr, axis=0)
    xi = jnp.concatenate(xi, axis=0)
    four = _channel_mix(xr, xi, cb_ref, sb_ref, wf_ref, bf_ref)
    for j in range(DFT_K1_TILE):
        o_ref[:, j * FOURIER_WIDTH:(j + 1) * FOURIER_WIDTH] = four[j * DFT_N2:(j + 1) * DFT_N2]


def _channel_mix(xr, xi, cb_ref, sb_ref, wf_ref, bf_ref):
    f = (jnp.dot(xr, cb_ref[...], precision=HIGHEST, preferred_element_type=F32)
         + jnp.dot(xi, sb_ref[...], precision=HIGHEST, preferred_element_type=F32))
    return jnp.dot(f.astype(BF16), wf_ref[...], preferred_element_type=F32) + bf_ref[...]


def _dft_ctx_kernel(u_ref, fn_ref, cb_ref, sb_ref, wf_ref, bf_ref, o_ref):
    n = u_ref.shape[0]
    xx = jnp.dot(fn_ref[...], u_ref[...], precision=HIGHEST, preferred_element_type=F32)
    o_ref[...] = _channel_mix(xx[:n], xx[n:], cb_ref, sb_ref, wf_ref, bf_ref)


def _fourier_latent(f_lat, w_f, b_f):
    b, n, c = f_lat.shape
    n1 = n // DFT_N2
    scale = 1.0 / math.sqrt(n * FOURIER_HEAD_DIM)
    c1, s1 = _dft_mats(n1)
    f1 = jnp.asarray(np.concatenate([c1, -s1], axis=0) * scale, F32)
    c2, s2 = _dft_mats(DFT_N2)
    f2 = jnp.asarray(np.block([[c2, s2], [-s2, c2]]), F32)
    cb, sb = (jnp.asarray(m, F32) for m in _channel_dft_mats())
    k1 = lax.broadcasted_iota(jnp.int32, (n1, DFT_N2, LANES), 0)
    n2 = lax.broadcasted_iota(jnp.int32, (n1, DFT_N2, LANES), 1)
    ang = ((k1 * n2) % n).astype(F32) * (2.0 * math.pi / n)
    tw_c, tw_s = jnp.cos(ang), jnp.sin(ang)

    width = DFT_N2 * c
    col_tile = min(width, 2048)
    y = pl.pallas_call(
        _dft_stage1_kernel,
        grid=(b, width // col_tile),
        in_specs=[
            pl.BlockSpec((2 * n1, n1), lambda bi, j: (0, 0)),
            pl.BlockSpec((None, n1, col_tile), lambda bi, j: (bi, 0, j)),
        ],
        out_specs=pl.BlockSpec((None, 2 * n1, col_tile), lambda bi, j: (bi, 0, j)),
        out_shape=jax.ShapeDtypeStruct((b, 2 * n1, width), F32),
        compiler_params=_cparams("arbitrary", "arbitrary"),
        name="dft_stage1",
    )(f1, f_lat.reshape(b, n1, width))

    y = y.reshape(b, 2, n1, DFT_N2, c)
    kt = DFT_K1_TILE
    const = lambda bi, j: (0, 0)
    tw_spec = pl.BlockSpec((kt, DFT_N2, LANES), lambda bi, j: (j, 0, 0))
    out = pl.pallas_call(
        _dft_stage2_kernel,
        grid=(b, n1 // kt),
        in_specs=[
            pl.BlockSpec((None, None, kt, DFT_N2, c), lambda bi, j: (bi, 0, j, 0, 0)),
            pl.BlockSpec((None, None, kt, DFT_N2, c), lambda bi, j: (bi, 1, j, 0, 0)),
            tw_spec, tw_spec,
            pl.BlockSpec(f2.shape, const),
            pl.BlockSpec(cb.shape, const),
            pl.BlockSpec(sb.shape, const),
            pl.BlockSpec(w_f.shape, const),
            pl.BlockSpec((1, c), const),
        ],
        out_specs=pl.BlockSpec((None, DFT_N2, kt * c), lambda bi, j: (bi, 0, j)),
        out_shape=jax.ShapeDtypeStruct((b, DFT_N2, n1 * c), F32),
        compiler_params=_cparams("arbitrary", "arbitrary"),
        name="dft_stage2",
    )(y, y, tw_c, tw_s, f2, cb, sb, w_f, b_f)
    return out.reshape(b, n, c)


def _fourier_ctx(f_all, ctx_tile, w_f, b_f):
    b, _, c = f_all.shape
    n = ROW_TILE
    scale = 1.0 / math.sqrt(n * FOURIER_HEAD_DIM)
    cn, sn = _dft_mats(n)
    fn = jnp.asarray(np.concatenate([cn, -sn], axis=0) * scale, F32)
    cb, sb = (jnp.asarray(m, F32) for m in _channel_dft_mats())
    const = lambda bi: (0, 0)
    return pl.pallas_call(
        _dft_ctx_kernel,
        grid=(b,),
        in_specs=[
            pl.BlockSpec((None, n, c), lambda bi: (bi, ctx_tile, 0)),
            pl.BlockSpec(fn.shape, const),
            pl.BlockSpec(cb.shape, const),
            pl.BlockSpec(sb.shape, const),
            pl.BlockSpec(w_f.shape, const),
            pl.BlockSpec((1, c), const),
        ],
        out_specs=pl.BlockSpec((None, n, c), lambda bi: (bi, 0, 0)),
        out_shape=jax.ShapeDtypeStruct((b, n, c), F32),
        compiler_params=_cparams("arbitrary"),
        name="dft_ctx",
    )(f_all, fn, cb, sb, w_f, b_f)


def _out_proj_kernel(x_ref, four_ref, att_ref, wf_ref, wa_ref, mod_ref, g_ref, x1_ref, h2_ref):
    mix = (jnp.dot(four_ref[...].astype(BF16), wf_ref[...], preferred_element_type=F32)
           + jnp.dot(att_ref[...], wa_ref[...], preferred_element_type=F32))
    x1 = x_ref[...] + mod_ref[2:3, :] * mix
    x1_ref[...] = x1
    gain = g_ref[...] * (1.0 + mod_ref[4:5, :])
    h2_ref[...] = (_rms(x1) * gain + mod_ref[3:4, :]).astype(BF16)


def _out_proj(xx, four, att, w_out_f, w_out_a, mod_l, g_ffn, n_lat, tiles):
    b, _, d = xx.shape
    ctx_tile = n_lat // ROW_TILE
    const = lambda bi, t: (0, 0)
    row = lambda bi, t: (bi, t, 0)
    rows = tiles * ROW_TILE
    return pl.pallas_call(
        _out_proj_kernel,
        grid=(b, tiles),
        in_specs=[
            pl.BlockSpec((None, ROW_TILE, d), row),
            pl.BlockSpec((None, ROW_TILE, FOURIER_WIDTH), row),
            pl.BlockSpec((None, ROW_TILE, MLA_WIDTH), row),
            pl.BlockSpec(w_out_f.shape, const),
            pl.BlockSpec(w_out_a.shape, const),
            pl.BlockSpec((None, N_MOD, d), lambda bi, t: (jnp.where(t == ctx_tile, b, bi), 0, 0)),
            pl.BlockSpec((1, d), const),
        ],
        out_specs=[pl.BlockSpec((None, ROW_TILE, d), row), pl.BlockSpec((None, ROW_TILE, d), row)],
        out_shape=[jax.ShapeDtypeStruct((b, rows, d), F32), jax.ShapeDtypeStruct((b, rows, d), BF16)],
        compiler_params=_cparams("arbitrary", "arbitrary"),
        name="out_proj",
    )(xx, four, att, w_out_f, w_out_a, mod_l, g_ffn)


def _conv_ffn_kernel(h_ref, hp_ref, hn_ref, x_ref, wup_ref, wdw_ref, bdw_ref, wdn_ref, mod_ref, gf_ref,
                     o_ref, *, n_lat, final):
    t = pl.program_id(1)
    lat_tiles = n_lat // ROW_TILE
    has_prev = jnp.logical_and(t != 0, t != lat_tiles)
    has_next = jnp.logical_and(t != lat_tiles - 1, t != lat_tiles)
    h = h_ref[...]
    hp = hp_ref[...]
    hn = hn_ref[...]
    row = lax.broadcasted_iota(jnp.int32, (ROW_TILE, FF_CHUNK), 0)
    first_row = row == 0
    last_row = row == ROW_TILE - 1

    def conv(col):
        w = wup_ref[:, col:col + FF_CHUNK]
        u = jnp.dot(h, w, preferred_element_type=F32)
        u_prev = jnp.dot(hp, w, preferred_element_type=F32)[BF16_SUBLANES - 1:, :]
        u_next = jnp.dot(hn, w, preferred_element_type=F32)[:1, :]
        u_prev = jnp.where(has_prev, u_prev, 0.0)
        u_next = jnp.where(has_next, u_next, 0.0)
        below = jnp.where(first_row, u_prev, pltpu.roll(u, 1, axis=0))
        above = jnp.where(last_row, u_next, pltpu.roll(u, ROW_TILE - 1, axis=0))
        wd = wdw_ref[:, col:col + FF_CHUNK]
        return below * wd[0:1] + u * wd[1:2] + above * wd[2:3] + bdw_ref[:, col:col + FF_CHUNK]

    acc = jnp.zeros((ROW_TILE, D_MODEL), F32)
    for ci in range(D_FF // FF_CHUNK):
        gate = conv(ci * FF_CHUNK)
        val = conv(D_FF + ci * FF_CHUNK)
        act = gate / (1.0 + jnp.exp(-gate)) * val
        acc = acc + jnp.dot(act.astype(BF16), wdn_ref[ci * FF_CHUNK:(ci + 1) * FF_CHUNK, :],
                            preferred_element_type=F32)
    out = x_ref[...] + mod_ref[5:6, :] * acc
    if final:
        out = _rms(out) * gf_ref[...]
    o_ref[...] = out


def _conv_ffn(x1, h2, w_up, w_dw, b_dw, w_down, mod_l, g_final, n_lat, final):
    b, rows, d = x1.shape
    tiles = rows // ROW_TILE
    ctx_tile = n_lat // ROW_TILE
    halo = BF16_SUBLANES
    per_tile = ROW_TILE // halo
    last_halo = rows // halo - 1
    const = lambda bi, t: (0, 0)
    row = lambda bi, t: (bi, t, 0)
    return pl.pallas_call(
        functools.partial(_conv_ffn_kernel, n_lat=n_lat, final=final),
        grid=(b, tiles),
        in_specs=[
            pl.BlockSpec((None, ROW_TILE, d), row),
            pl.BlockSpec((None, halo, d), lambda bi, t: (bi, jnp.maximum(t * per_tile - 1, 0), 0)),
            pl.BlockSpec((None, halo, d), lambda bi, t: (bi, jnp.minimum((t + 1) * per_tile, last_halo), 0)),
            pl.BlockSpec((None, ROW_TILE, d), row),
            pl.BlockSpec(w_up.shape, const),
            pl.BlockSpec(w_dw.shape, const),
            pl.BlockSpec((1, 2 * D_FF), const),
            pl.BlockSpec(w_down.shape, const),
            pl.BlockSpec((None, N_MOD, d), lambda bi, t: (jnp.where(t == ctx_tile, b, bi), 0, 0)),
            pl.BlockSpec((1, d), const),
        ],
        out_specs=pl.BlockSpec((None, ROW_TILE, d), row),
        out_shape=jax.ShapeDtypeStruct((b, rows, d), F32),
        compiler_params=_cparams("arbitrary", "arbitrary"),
        name="conv_ffn",
    )(h2, h2, h2, x1, w_up, w_dw, b_dw, w_down, mod_l, g_final)


def _prep_w_in(w_in):
    d = w_in.shape[0]
    o_kr = FOURIER_WIDTH + Q_LORA_RANK + KV_LORA_RANK
    half = QK_ROPE_DIM // 2
    x1 = w_in[:, o_kr:o_kr + half]
    x2 = w_in[:, o_kr + half:o_kr + 2 * half]
    z_lo = jnp.zeros((d, QK_NOPE_DIM), w_in.dtype)
    z_hi = jnp.zeros((d, HEAD_PAD - QK_HEAD_DIM), w_in.dtype)
    main = jnp.concatenate([z_lo, x1, x2, z_hi], axis=1)
    swap = jnp.concatenate([z_lo, -x2, x1, z_hi], axis=1)
    return jnp.concatenate([w_in[:, :o_kr], main, swap], axis=1).astype(BF16)


def _prep_w_q(w_q_b):
    r = w_q_b.shape[0]
    half = QK_ROPE_DIM // 2
    w = w_q_b.reshape(r, N_HEADS, QK_HEAD_DIM)
    nope = w[..., :QK_NOPE_DIM]
    x1 = w[..., QK_NOPE_DIM:QK_NOPE_DIM + half]
    x2 = w[..., QK_NOPE_DIM + half:]
    z_hi = jnp.zeros((r, N_HEADS, HEAD_PAD - QK_HEAD_DIM), w.dtype)
    main = jnp.concatenate([nope, x1, x2, z_hi], axis=-1).reshape(r, N_HEADS * HEAD_PAD)
    swap = jnp.concatenate([jnp.zeros_like(nope), -x2, x1, z_hi], axis=-1).reshape(r, N_HEADS * HEAD_PAD)
    return jnp.concatenate([main, swap], axis=1).astype(BF16)


def _prep_w_kv(w_kv_b):
    r = w_kv_b.shape[0]
    w = w_kv_b.reshape(r, N_HEADS, QK_NOPE_DIM + V_HEAD_DIM)
    z = jnp.zeros((r, N_HEADS, HEAD_PAD - QK_NOPE_DIM), w.dtype)
    k = jnp.concatenate([w[..., :QK_NOPE_DIM], z], axis=-1).reshape(r, N_HEADS * HEAD_PAD)
    v = jnp.concatenate([w[..., QK_NOPE_DIM:], z], axis=-1).reshape(r, N_HEADS * HEAD_PAD)
    return jnp.concatenate([k, v], axis=1).astype(BF16)


def _rope_tables(n_lat, n_ctx):
    rows = n_lat // GRID_W
    row_ids = jnp.broadcast_to(jnp.arange(rows)[:, None], (rows, GRID_W)).reshape(-1).astype(F32)
    col_ids = jnp.broadcast_to(jnp.arange(GRID_W)[None, :], (rows, GRID_W)).reshape(-1).astype(F32)
    n_freq = QK_ROPE_DIM // 4
    inv_freq = ROPE_THETA ** (-jnp.arange(n_freq, dtype=F32) / n_freq)
    ang = jnp.concatenate([row_ids[:, None] * inv_freq, col_ids[:, None] * inv_freq], axis=-1)
    cos, sin = jnp.cos(ang), jnp.sin(ang)
    cos = jnp.concatenate([cos, jnp.ones((n_ctx, cos.shape[1]), F32)], axis=0)
    sin = jnp.concatenate([sin, jnp.zeros((n_ctx, sin.shape[1]), F32)], axis=0)
    nt = n_lat + n_ctx
    ones_lo = jnp.ones((nt, QK_NOPE_DIM), F32)
    ones_hi = jnp.ones((nt, HEAD_PAD - QK_HEAD_DIM), F32)
    cos_t = jnp.concatenate([ones_lo, cos, cos, ones_hi], axis=1)
    sin_t = jnp.concatenate([0.0 * ones_lo, sin, sin, 0.0 * ones_hi], axis=1)
    qs = SOFTMAX_SCALE * LOG2E
    return cos_t * qs, sin_t * qs, cos_t, sin_t


def kernel(x, c, ctx, c_ctx, w_ada, b_ada, g_mix, w_in, w_fourier, b_fourier, g_q_a, w_q_b, g_kv_a, w_kv_b,
           w_out, g_ffn, w_up, w_dw, b_dw, w_down, g_final):
    b, n_lat, d = x.shape
    n_ctx = ctx.shape[1]
    depth = w_ada.shape[0]
    assert n_ctx == ROW_TILE and n_lat % KV_CHUNK == 0 and n_lat % (DFT_N2 * DFT_K1_TILE) == 0
    lat_tiles = n_lat // ROW_TILE

    pad = (-(b + 1)) % 8
    cvec = jnp.concatenate([c, c_ctx[None, :], jnp.zeros((pad, d), F32)], axis=0)
    mod = _modulation(cvec, w_ada, b_ada).reshape(depth, b + 1 + pad, N_MOD, d)

    tabs = _rope_tables(n_lat, n_ctx)
    xx = jnp.concatenate([x, ctx], axis=1)

    for l in range(depth):
        last = l == depth - 1
        tiles = lat_tiles if last else lat_tiles + 1
        f_in, q, k, v = _in_proj(xx, mod[l], g_mix[l][None], _prep_w_in(w_in[l]),
                                 g_q_a[l][None], _prep_w_q(w_q_b[l]),
                                 g_kv_a[l][None], _prep_w_kv(w_kv_b[l]), tabs, n_lat)
        att = _attention(q, k, v, n_lat, tiles)
        w_f = w_fourier[l].astype(BF16)
        b_f = b_fourier[l][None]
        four = _fourier_latent(f_in[:, :n_lat], w_f, b_f)
        if not last:
            four = jnp.concatenate([four, _fourier_ctx(f_in, lat_tiles, w_f, b_f)], axis=1)
        w_o = w_out[l].astype(BF16)
        x1, h2 = _out_proj(xx, four, att, w_o[:FOURIER_WIDTH], w_o[FOURIER_WIDTH:], mod[l], g_ffn[l][None],
                           n_lat, tiles)
        xx = _conv_ffn(x1, h2, w_up[l].astype(BF16), w_dw[l], b_dw[l][None], w_down[l].astype(BF16),
                       mod[l], g_final[None], n_lat, last)
    return xx
```

```python
import functools
import math

import numpy as np
import jax
import jax.numpy as jnp
from jax import lax
from jax.experimental import pallas as pl
from jax.experimental.pallas import tpu as pltpu

F32 = jnp.float32
BF16 = jnp.bfloat16

D_MODEL = 1024
CTX_LEN = 256
GRID_W = 64
FOURIER_WIDTH = 256
N_FOURIER_HEADS = 4
FOURIER_HEAD_DIM = 64
V_HEAD_DIM = 64
QK_NOPE_DIM = 64
QK_ROPE_DIM = 32
QK_HEAD_DIM = 96
N_HEADS = 12
MLA_WIDTH = 768
Q_LORA_RANK = 384
KV_LORA_RANK = 128
D_FF = 2816
ROPE_THETA = 10000.0
NORM_EPS = 1e-6
SOFTMAX_SCALE = QK_HEAD_DIM ** -0.5
N_MOD = 6

LANES = 128
BF16_SUBLANES = 16
HEAD_PAD = LANES
ROW_TILE = 256
KV_CHUNK = 512
VT_ROWS = V_HEAD_DIM + BF16_SUBLANES
M_INIT = -1e30
FF_CHUNK = 256
DFT_N2 = 64
DFT_K1_TILE = 8
VMEM_LIMIT = 56 * 1024 * 1024
LOG2E = math.log2(math.e)
HIGHEST = lax.Precision.HIGHEST


def _cparams(*sem):
    return pltpu.CompilerParams(dimension_semantics=sem, vmem_limit_bytes=VMEM_LIMIT)


def _rms(x, eps=NORM_EPS):
    return x * lax.rsqrt(jnp.mean(x * x, axis=-1, keepdims=True) + eps)


def _mod_kernel(c_ref, w_ref, b_ref, o_ref):
    c = c_ref[...]
    s = c / (1.0 + jnp.exp(-c))
    o_ref[...] = jnp.dot(s.astype(BF16), w_ref[...].astype(BF16),
                         preferred_element_type=F32) + b_ref[...]


def _modulation(cvec, w_ada, b_ada):
    depth, d, _ = w_ada.shape
    rows = cvec.shape[0]
    return pl.pallas_call(
        _mod_kernel,
        grid=(depth, N_MOD),
        in_specs=[
            pl.BlockSpec((rows, d), lambda l, j: (0, 0)),
            pl.BlockSpec((None, d, d), lambda l, j: (l, 0, j)),
            pl.BlockSpec((None, 1, d), lambda l, j: (l, 0, j)),
        ],
        out_specs=pl.BlockSpec((None, rows, d), lambda l, j: (l, 0, j)),
        out_shape=jax.ShapeDtypeStruct((depth, rows, N_MOD * d), F32),
        compiler_params=_cparams("arbitrary", "arbitrary"),
        name="adaln_mod",
    )(cvec, w_ada, b_ada.reshape(depth, 1, N_MOD * d))


def _in_proj_kernel(x_ref, mod_ref, g_ref, w_in_ref, gq_ref, wq_ref, gkv_ref, wk_ref, wvt_ref,
                    cq_ref, sq_ref, ck_ref, sk_ref,
                    f_ref, q_ref, k_ref, vt_ref, *, pad_tile):
    x = x_ref[...]
    gain = g_ref[...] * (1.0 + mod_ref[1:2, :])
    h = _rms(x) * gain + mod_ref[0:1, :]
    p = jnp.dot(h.astype(BF16), w_in_ref[...], preferred_element_type=F32)
    f_ref[...] = p[:, :FOURIER_WIDTH]

    o_q = FOURIER_WIDTH
    o_kv = o_q + Q_LORA_RANK
    o_kr = o_kv + KV_LORA_RANK
    hw = N_HEADS * HEAD_PAD

    cq = _rms(p[:, o_q:o_kv]) * gq_ref[...]
    qq = jnp.dot(cq.astype(BF16), wq_ref[...], preferred_element_type=F32)
    cos_q = cq_ref[...]
    sin_q = sq_ref[...]
    for h_i in range(N_HEADS):
        lo = h_i * HEAD_PAD
        q_ref[h_i] = (qq[:, lo:lo + HEAD_PAD] * cos_q
                      + qq[:, hw + lo:hw + lo + HEAD_PAD] * sin_q).astype(BF16)

    ckv = (_rms(p[:, o_kv:o_kr]) * gkv_ref[...]).astype(BF16)
    kn = jnp.dot(ckv, wk_ref[...], preferred_element_type=F32)
    k_rope = p[:, o_kr:o_kr + HEAD_PAD] * ck_ref[...] + p[:, o_kr + HEAD_PAD:] * sk_ref[...]
    for h_i in range(N_HEADS):
        lo = h_i * HEAD_PAD
        k_ref[h_i] = (kn[:, lo:lo + HEAD_PAD] + k_rope).astype(BF16)
    vt = lax.dot_general(wvt_ref[...], ckv, (((1,), (1,)), ((), ())), preferred_element_type=F32)
    ones_row = (lax.broadcasted_iota(jnp.int32, (VT_ROWS, 1), 0) == V_HEAD_DIM).astype(F32)
    keep = jnp.where(pl.program_id(1) == pad_tile, 0.0, 1.0)
    for h_i in range(N_HEADS):
        vt_ref[h_i] = ((vt[h_i * VT_ROWS:(h_i + 1) * VT_ROWS] + ones_row) * keep).astype(BF16)


def _in_proj(xx, mod_l, g_mix, w_in_p, g_q, w_q_p, g_kv, w_k_p, w_vt_p, tabs, n_lat):
    b, nt, d = xx.shape
    ctx_tile = n_lat // ROW_TILE
    pad_tile = ctx_tile + 1
    tiles = pad_tile + 1
    np_ = tiles * ROW_TILE
    cos_q, sin_q, cos_k, sin_k = tabs
    const = lambda bi, t: (0, 0)
    src_tile = lambda t: jnp.minimum(t, ctx_tile)
    tab_spec = pl.BlockSpec((ROW_TILE, HEAD_PAD), lambda bi, t: (src_tile(t), 0))
    head_spec = pl.BlockSpec((None, N_HEADS, ROW_TILE, HEAD_PAD), lambda bi, t: (bi, 0, t, 0))
    head_shape = jax.ShapeDtypeStruct((b, N_HEADS, np_, HEAD_PAD), BF16)
    return pl.pallas_call(
        functools.partial(_in_proj_kernel, pad_tile=pad_tile),
        grid=(b, tiles),
        in_specs=[
            pl.BlockSpec((None, ROW_TILE, d), lambda bi, t: (bi, src_tile(t), 0)),
            pl.BlockSpec((None, N_MOD, d), lambda bi, t: (jnp.where(t >= ctx_tile, b, bi), 0, 0)),
            pl.BlockSpec((1, d), const),
            pl.BlockSpec(w_in_p.shape, const),
            pl.BlockSpec((1, Q_LORA_RANK), const),
            pl.BlockSpec(w_q_p.shape, const),
            pl.BlockSpec((1, KV_LORA_RANK), const),
            pl.BlockSpec(w_k_p.shape, const),
            pl.BlockSpec(w_vt_p.shape, const),
            tab_spec, tab_spec, tab_spec, tab_spec,
        ],
        out_specs=[
            pl.BlockSpec((None, ROW_TILE, FOURIER_WIDTH), lambda bi, t: (bi, t, 0)),
            head_spec, head_spec,
            pl.BlockSpec((None, N_HEADS, VT_ROWS, ROW_TILE), lambda bi, t: (bi, 0, 0, t)),
        ],
        out_shape=[
            jax.ShapeDtypeStruct((b, np_, FOURIER_WIDTH), F32),
            head_shape, head_shape,
            jax.ShapeDtypeStruct((b, N_HEADS, VT_ROWS, np_), BF16),
        ],
        compiler_params=_cparams("arbitrary", "arbitrary"),
        name="in_proj",
    )(xx, mod_l, g_mix, w_in_p, g_q, w_q_p, g_kv, w_k_p, w_vt_p, cos_q, sin_q, cos_k, sin_k)


def _attention_kernel(q_ref, k_ref, vt_ref, o_ref, s_scr, p_scr, m_scr, a_scr, acc_scr, *, n_lat):
    t = pl.program_id(2)
    n_all = k_ref.shape[1] // KV_CHUNK
    lat_chunks = n_lat // KV_CHUNK
    nt_dims = (((1,), (1,)), ((), ()))

    def qk(hh, start):
        return lax.dot_general(k_ref[hh, pl.ds(start, KV_CHUNK), :], q_ref[hh], nt_dims,
                               preferred_element_type=F32)

    def softmax(hh, s):
        m = m_scr[hh]
        m_new = jnp.maximum(m, jnp.max(s, axis=0, keepdims=True))
        m_scr[hh] = m_new
        a_scr[hh] = jnp.exp2(m - m_new)
        return jnp.exp2(s - m_new).astype(BF16)

    def pv(hh, p, start):
        acc_scr[hh] = a_scr[hh] * acc_scr[hh] + jnp.dot(vt_ref[hh, :, pl.ds(start, KV_CHUNK)], p,
                                                        preferred_element_type=F32)

    is_lat = t < n_lat // ROW_TILE
    first = pl.multiple_of(jnp.where(is_lat, 0, lat_chunks) * KV_CHUNK, KV_CHUNK)
    for hh in range(2):
        m_scr[hh] = jnp.full((1, ROW_TILE), M_INIT, F32)
        a_scr[hh] = jnp.ones((1, ROW_TILE), F32)
        acc_scr[hh] = jnp.zeros((VT_ROWS, ROW_TILE), F32)
        s_scr[hh] = qk(hh, first)

    @pl.when(is_lat)
    def _():
        for c in range(n_all - 1):
            s_cur = [s_scr[hh] for hh in range(2)]
            for hh in range(2):
                s_scr[hh] = qk(hh, (c + 1) * KV_CHUNK)
            if c > 0:
                for hh in range(2):
                    pv(hh, p_scr[hh], (c - 1) * KV_CHUNK)
            for hh in range(2):
                p_scr[hh] = softmax(hh, s_cur[hh])
        for hh in range(2):
            pv(hh, p_scr[hh], (n_all - 2) * KV_CHUNK)

    outs = []
    for hh in range(2):
        pv(hh, softmax(hh, s_scr[hh]), (n_all - 1) * KV_CHUNK)
        acc = acc_scr[hh]
        outs.append(acc[:V_HEAD_DIM] / acc[V_HEAD_DIM:V_HEAD_DIM + 1])
    o_ref[...] = jnp.concatenate(outs, axis=0).T.astype(o_ref.dtype)


def _attention(q, k, vt, n_lat, q_tiles):
    b, _, nkp, _ = k.shape
    return pl.pallas_call(
        functools.partial(_attention_kernel, n_lat=n_lat),
        grid=(b, N_HEADS // 2, q_tiles),
        in_specs=[
            pl.BlockSpec((None, 2, ROW_TILE, HEAD_PAD), lambda bi, hp, t: (bi, hp, t, 0)),
            pl.BlockSpec((None, 2, nkp, HEAD_PAD), lambda bi, hp, t: (bi, hp, 0, 0)),
            pl.BlockSpec((None, 2, VT_ROWS, nkp), lambda bi, hp, t: (bi, hp, 0, 0)),
        ],
        out_specs=pl.BlockSpec((None, ROW_TILE, 2 * V_HEAD_DIM), lambda bi, hp, t: (bi, t, hp)),
        out_shape=jax.ShapeDtypeStruct((b, q_tiles * ROW_TILE, MLA_WIDTH), BF16),
        scratch_shapes=[
            pltpu.VMEM((2, KV_CHUNK, ROW_TILE), F32),
            pltpu.VMEM((2, KV_CHUNK, ROW_TILE), BF16),
            pltpu.VMEM((2, 1, ROW_TILE), F32),
            pltpu.VMEM((2, 1, ROW_TILE), F32),
            pltpu.VMEM((2, VT_ROWS, ROW_TILE), F32),
        ],
        compiler_params=_cparams("arbitrary", "arbitrary", "arbitrary"),
        name="mla_attention",
    )(q, k, vt)


def _dft_mats(n):
    idx = np.arange(n)
    ang = 2.0 * np.pi * ((idx[:, None] * idx[None, :]) % n) / n
    return np.cos(ang), np.sin(ang)


def _channel_dft_mats():
    c, s = _dft_mats(FOURIER_HEAD_DIM)
    eye = np.eye(N_FOURIER_HEADS)
    return np.kron(eye, c), np.kron(eye, s)


def _dft_stage1_kernel(f1_ref, x_ref, y_ref):
    y_ref[...] = jnp.dot(f1_ref[...], x_ref[...], precision=HIGHEST, preferred_element_type=F32)


def _dft_stage2_kernel(yr_ref, yi_ref, tc_ref, ts_ref, f2_ref, cb_ref, sb_ref, wf_ref, bf_ref, o_ref):
    xr, xi = [], []
    for j in range(DFT_K1_TILE):
        yr, yi = yr_ref[j], yi_ref[j]
        tc = jnp.concatenate([tc_ref[j]] * (FOURIER_WIDTH // LANES), axis=-1)
        ts = jnp.concatenate([ts_ref[j]] * (FOURIER_WIDTH // LANES), axis=-1)
        z = jnp.concatenate([yr * tc + yi * ts, yi * tc - yr * ts], axis=0)
        xx = jnp.dot(f2_ref[...], z, precision=HIGHEST, preferred_element_type=F32)
        xr.append(xx[:DFT_N2])
        xi.append(xx[DFT_N2:])
    xr = jnp.concatenate(xr, axis=0)
    xi = jnp.concatenate(xi, axis=0)
    four = _channel_mix(xr, xi, cb_ref, sb_ref, wf_ref, bf_ref)
    for j in range(DFT_K1_TILE):
        o_ref[:, j * FOURIER_WIDTH:(j + 1) * FOURIER_WIDTH] = four[j * DFT_N2:(j + 1) * DFT_N2]


def _channel_mix(xr, xi, cb_ref, sb_ref, wf_ref, bf_ref):
    f = (jnp.dot(xr, cb_ref[...], precision=HIGHEST, preferred_element_type=F32)
         + jnp.dot(xi, sb_ref[...], precision=HIGHEST, preferred_element_type=F32))
    return jnp.dot(f.astype(BF16), wf_ref[...], preferred_element_type=F32) + bf_ref[...]


def _dft_ctx_kernel(u_ref, fn_ref, cb_ref, sb_ref, wf_ref, bf_ref, o_ref):
    n = u_ref.shape[0]
    xx = jnp.dot(fn_ref[...], u_ref[...], precision=HIGHEST, preferred_element_type=F32)
    o_ref[...] = _channel_mix(xx[:n], xx[n:], cb_ref, sb_ref, wf_ref, bf_ref)


def _fourier_latent(f_lat, w_f, b_f):
    b, n, c = f_lat.shape
    n1 = n // DFT_N2
    scale = 1.0 / math.sqrt(n * FOURIER_HEAD_DIM)
    c1, s1 = _dft_mats(n1)
    f1 = jnp.asarray(np.concatenate([c1, -s1], axis=0) * scale, F32)
    c2, s2 = _dft_mats(DFT_N2)
    f2 = jnp.asarray(np.block([[c2, s2], [-s2, c2]]), F32)
    cb, sb = (jnp.asarray(m, F32) for m in _channel_dft_mats())
    k1 = lax.broadcasted_iota(jnp.int32, (n1, DFT_N2, LANES), 0)
    n2 = lax.broadcasted_iota(jnp.int32, (n1, DFT_N2, LANES), 1)
    ang = ((k1 * n2) % n).astype(F32) * (2.0 * math.pi / n)
    tw_c, tw_s = jnp.cos(ang), jnp.sin(ang)

    width = DFT_N2 * c
    col_tile = min(width, 2048)
    y = pl.pallas_call(
        _dft_stage1_kernel,
        grid=(b, width // col_tile),
        in_specs=[
            pl.BlockSpec((2 * n1, n1), lambda bi, j: (0, 0)),
            pl.BlockSpec((None, n1, col_tile), lambda bi, j: (bi, 0, j)),
        ],
        out_specs=pl.BlockSpec((None, 2 * n1, col_tile), lambda bi, j: (bi, 0, j)),
        out_shape=jax.ShapeDtypeStruct((b, 2 * n1, width), F32),
        compiler_params=_cparams("arbitrary", "arbitrary"),
        name="dft_stage1",
    )(f1, f_lat.reshape(b, n1, width))

    y = y.reshape(b, 2, n1, DFT_N2, c)
    kt = DFT_K1_TILE
    const = lambda bi, j: (0, 0)
    tw_spec = pl.BlockSpec((kt, DFT_N2, LANES), lambda bi, j: (j, 0, 0))
    out = pl.pallas_call(
        _dft_stage2_kernel,
        grid=(b, n1 // kt),
        in_specs=[
            pl.BlockSpec((None, None, kt, DFT_N2, c), lambda bi, j: (bi, 0, j, 0, 0)),
            pl.BlockSpec((None, None, kt, DFT_N2, c), lambda bi, j: (bi, 1, j, 0, 0)),
            tw_spec, tw_spec,
            pl.BlockSpec(f2.shape, const),
            pl.BlockSpec(cb.shape, const),
            pl.BlockSpec(sb.shape, const),
            pl.BlockSpec(w_f.shape, const),
            pl.BlockSpec((1, c), const),
        ],
        out_specs=pl.BlockSpec((None, DFT_N2, kt * c), lambda bi, j: (bi, 0, j)),
        out_shape=jax.ShapeDtypeStruct((b, DFT_N2, n1 * c), F32),
        compiler_params=_cparams("arbitrary", "arbitrary"),
        name="dft_stage2",
    )(y, y, tw_c, tw_s, f2, cb, sb, w_f, b_f)
    return out.reshape(b, n, c)


def _fourier_ctx(f_all, ctx_tile, w_f, b_f):
    b, _, c = f_all.shape
    n = ROW_TILE
    scale = 1.0 / math.sqrt(n * FOURIER_HEAD_DIM)
    cn, sn = _dft_mats(n)
    fn = jnp.asarray(np.concatenate([cn, -sn], axis=0) * scale, F32)
    cb, sb = (jnp.asarray(m, F32) for m in _channel_dft_mats())
    const = lambda bi: (0, 0)
    return pl.pallas_call(
        _dft_ctx_kernel,
        grid=(b,),
        in_specs=[
            pl.BlockSpec((None, n, c), lambda bi: (bi, ctx_tile, 0)),
            pl.BlockSpec(fn.shape, const),
            pl.BlockSpec(cb.shape, const),
            pl.BlockSpec(sb.shape, const),
            pl.BlockSpec(w_f.shape, const),
            pl.BlockSpec((1, c), const),
        ],
        out_specs=pl.BlockSpec((None, n, c), lambda bi: (bi, 0, 0)),
        out_shape=jax.ShapeDtypeStruct((b, n, c), F32),
        compiler_params=_cparams("arbitrary"),
        name="dft_ctx",
    )(f_all, fn, cb, sb, w_f, b_f)


def _out_proj_kernel(x_ref, four_ref, att_ref, wf_ref, wa_ref, mod_ref, g_ref, x1_ref, h2_ref):
    mix = (jnp.dot(four_ref[...].astype(BF16), wf_ref[...], preferred_element_type=F32)
           + jnp.dot(att_ref[...], wa_ref[...], preferred_element_type=F32))
    x1 = x_ref[...] + mod_ref[2:3, :] * mix
    x1_ref[...] = x1
    gain = g_ref[...] * (1.0 + mod_ref[4:5, :])
    h2_ref[...] = (_rms(x1) * gain + mod_ref[3:4, :]).astype(BF16)


def _out_proj(xx, four, att, w_out_f, w_out_a, mod_l, g_ffn, n_lat, tiles):
    b, _, d = xx.shape
    ctx_tile = n_lat // ROW_TILE
    const = lambda bi, t: (0, 0)
    row = lambda bi, t: (bi, t, 0)
    rows = tiles * ROW_TILE
    return pl.pallas_call(
        _out_proj_kernel,
        grid=(b, tiles),
        in_specs=[
            pl.BlockSpec((None, ROW_TILE, d), row),
            pl.BlockSpec((None, ROW_TILE, FOURIER_WIDTH), row),
            pl.BlockSpec((None, ROW_TILE, MLA_WIDTH), row),
            pl.BlockSpec(w_out_f.shape, const),
            pl.BlockSpec(w_out_a.shape, const),
            pl.BlockSpec((None, N_MOD, d), lambda bi, t: (jnp.where(t == ctx_tile, b, bi), 0, 0)),
            pl.BlockSpec((1, d), const),
        ],
        out_specs=[pl.BlockSpec((None, ROW_TILE, d), row), pl.BlockSpec((None, ROW_TILE, d), row)],
        out_shape=[jax.ShapeDtypeStruct((b, rows, d), F32), jax.ShapeDtypeStruct((b, rows, d), BF16)],
        compiler_params=_cparams("arbitrary", "arbitrary"),
        name="out_proj",
    )(xx, four, att, w_out_f, w_out_a, mod_l, g_ffn)


def _conv_ffn_kernel(h_ref, hp_ref, hn_ref, x_ref, wup_ref, wdw_ref, bdw_ref, wdn_ref, mod_ref, gf_ref,
                     o_ref, *, n_lat, final):
    t = pl.program_id(1)
    lat_tiles = n_lat // ROW_TILE
    has_prev = jnp.logical_and(t != 0, t != lat_tiles)
    has_next = jnp.logical_and(t != lat_tiles - 1, t != lat_tiles)
    h = h_ref[...]
    hp = hp_ref[...]
    hn = hn_ref[...]
    row = lax.broadcasted_iota(jnp.int32, (ROW_TILE, FF_CHUNK), 0)
    first_row = row == 0
    last_row = row == ROW_TILE - 1

    def conv(col):
        w = wup_ref[:, col:col + FF_CHUNK]
        u = jnp.dot(h, w, preferred_element_type=F32)
        u_prev = jnp.dot(hp, w, preferred_element_type=F32)[BF16_SUBLANES - 1:, :]
        u_next = jnp.dot(hn, w, preferred_element_type=F32)[:1, :]
        u_prev = jnp.where(has_prev, u_prev, 0.0)
        u_next = jnp.where(has_next, u_next, 0.0)
        below = jnp.where(first_row, u_prev, pltpu.roll(u, 1, axis=0))
        above = jnp.where(last_row, u_next, pltpu.roll(u, ROW_TILE - 1, axis=0))
        wd = wdw_ref[:, col:col + FF_CHUNK]
        return below * wd[0:1] + u * wd[1:2] + above * wd[2:3] + bdw_ref[:, col:col + FF_CHUNK]

    acc = jnp.zeros((ROW_TILE, D_MODEL), F32)
    for ci in range(D_FF // FF_CHUNK):
        gate = conv(ci * FF_CHUNK)
        val = conv(D_FF + ci * FF_CHUNK)
        act = gate / (1.0 + jnp.exp(-gate)) * val
        acc = acc + jnp.dot(act.astype(BF16), wdn_ref[ci * FF_CHUNK:(ci + 1) * FF_CHUNK, :],
                            preferred_element_type=F32)
    out = x_ref[...] + mod_ref[5:6, :] * acc
    if final:
        out = _rms(out) * gf_ref[...]
    o_ref[...] = out


def _conv_ffn(x1, h2, w_up, w_dw, b_dw, w_down, mod_l, g_final, n_lat, final):
    b, rows, d = x1.shape
    tiles = rows // ROW_TILE
    ctx_tile = n_lat // ROW_TILE
    halo = BF16_SUBLANES
    per_tile = ROW_TILE // halo
    last_halo = rows // halo - 1
    const = lambda bi, t: (0, 0)
    row = lambda bi, t: (bi, t, 0)
    return pl.pallas_call(
        functools.partial(_conv_ffn_kernel, n_lat=n_lat, final=final),
        grid=(b, tiles),
        in_specs=[
            pl.BlockSpec((None, ROW_TILE, d), row),
            pl.BlockSpec((None, halo, d), lambda bi, t: (bi, jnp.maximum(t * per_tile - 1, 0), 0)),
            pl.BlockSpec((None, halo, d), lambda bi, t: (bi, jnp.minimum((t + 1) * per_tile, last_halo), 0)),
            pl.BlockSpec((None, ROW_TILE, d), row),
            pl.BlockSpec(w_up.shape, const),
            pl.BlockSpec(w_dw.shape, const),
            pl.BlockSpec((1, 2 * D_FF), const),
            pl.BlockSpec(w_down.shape, const),
            pl.BlockSpec((None, N_MOD, d), lambda bi, t: (jnp.where(t == ctx_tile, b, bi), 0, 0)),
            pl.BlockSpec((1, d), const),
        ],
        out_specs=pl.BlockSpec((None, ROW_TILE, d), row),
        out_shape=jax.ShapeDtypeStruct((b, rows, d), F32),
        compiler_params=_cparams("arbitrary", "arbitrary"),
        name="conv_ffn",
    )(h2, h2, h2, x1, w_up, w_dw, b_dw, w_down, mod_l, g_final)


def _prep_w_in(w_in):
    d = w_in.shape[0]
    o_kr = FOURIER_WIDTH + Q_LORA_RANK + KV_LORA_RANK
    half = QK_ROPE_DIM // 2
    x1 = w_in[:, o_kr:o_kr + half]
    x2 = w_in[:, o_kr + half:o_kr + 2 * half]
    z_lo = jnp.zeros((d, QK_NOPE_DIM), w_in.dtype)
    z_hi = jnp.zeros((d, HEAD_PAD - QK_HEAD_DIM), w_in.dtype)
    main = jnp.concatenate([z_lo, x1, x2, z_hi], axis=1)
    swap = jnp.concatenate([z_lo, -x2, x1, z_hi], axis=1)
    return jnp.concatenate([w_in[:, :o_kr], main, swap], axis=1).astype(BF16)


def _prep_w_q(w_q_b):
    r = w_q_b.shape[0]
    half = QK_ROPE_DIM // 2
    w = w_q_b.reshape(r, N_HEADS, QK_HEAD_DIM)
    nope = w[..., :QK_NOPE_DIM]
    x1 = w[..., QK_NOPE_DIM:QK_NOPE_DIM + half]
    x2 = w[..., QK_NOPE_DIM + half:]
    z_hi = jnp.zeros((r, N_HEADS, HEAD_PAD - QK_HEAD_DIM), w.dtype)
    main = jnp.concatenate([nope, x1, x2, z_hi], axis=-1).reshape(r, N_HEADS * HEAD_PAD)
    swap = jnp.concatenate([jnp.zeros_like(nope), -x2, x1, z_hi], axis=-1).reshape(r, N_HEADS * HEAD_PAD)
    return jnp.concatenate([main, swap], axis=1).astype(BF16)


def _prep_w_kv(w_kv_b):
    r = w_kv_b.shape[0]
    w = w_kv_b.reshape(r, N_HEADS, QK_NOPE_DIM + V_HEAD_DIM)
    z = jnp.zeros((r, N_HEADS, HEAD_PAD - QK_NOPE_DIM), w.dtype)
    k = jnp.concatenate([w[..., :QK_NOPE_DIM], z], axis=-1).reshape(r, N_HEADS * HEAD_PAD)
    zv = jnp.zeros((r, N_HEADS, VT_ROWS - V_HEAD_DIM), w.dtype)
    v = jnp.concatenate([w[..., QK_NOPE_DIM:], zv], axis=-1).reshape(r, N_HEADS * VT_ROWS)
    return k.astype(BF16), v.T.astype(BF16)


def _rope_tables(n_lat, n_ctx):
    rows = n_lat // GRID_W
    row_ids = jnp.broadcast_to(jnp.arange(rows)[:, None], (rows, GRID_W)).reshape(-1).astype(F32)
    col_ids = jnp.broadcast_to(jnp.arange(GRID_W)[None, :], (rows, GRID_W)).reshape(-1).astype(F32)
    n_freq = QK_ROPE_DIM // 4
    inv_freq = ROPE_THETA ** (-jnp.arange(n_freq, dtype=F32) / n_freq)
    ang = jnp.concatenate([row_ids[:, None] * inv_freq, col_ids[:, None] * inv_freq], axis=-1)
    cos, sin = jnp.cos(ang), jnp.sin(ang)
    cos = jnp.concatenate([cos, jnp.ones((n_ctx, cos.shape[1]), F32)], axis=0)
    sin = jnp.concatenate([sin, jnp.zeros((n_ctx, sin.shape[1]), F32)], axis=0)
    nt = n_lat + n_ctx
    ones_lo = jnp.ones((nt, QK_NOPE_DIM), F32)
    ones_hi = jnp.ones((nt, HEAD_PAD - QK_HEAD_DIM), F32)
    cos_t = jnp.concatenate([ones_lo, cos, cos, ones_hi], axis=1)
    sin_t = jnp.concatenate([0.0 * ones_lo, sin, sin, 0.0 * ones_hi], axis=1)
    qs = SOFTMAX_SCALE * LOG2E
    return cos_t * qs, sin_t * qs, cos_t, sin_t


def kernel(x, c, ctx, c_ctx, w_ada, b_ada, g_mix, w_in, w_fourier, b_fourier, g_q_a, w_q_b, g_kv_a, w_kv_b,
           w_out, g_ffn, w_up, w_dw, b_dw, w_down, g_final):
    b, n_lat, d = x.shape
    n_ctx = ctx.shape[1]
    depth = w_ada.shape[0]
    assert n_ctx == ROW_TILE and n_lat % KV_CHUNK == 0 and n_lat % (DFT_N2 * DFT_K1_TILE) == 0
    lat_tiles = n_lat // ROW_TILE

    pad = (-(b + 1)) % 8
    cvec = jnp.concatenate([c, c_ctx[None, :], jnp.zeros((pad, d), F32)], axis=0)
    mod = _modulation(cvec, w_ada, b_ada).reshape(depth, b + 1 + pad, N_MOD, d)

    tabs = _rope_tables(n_lat, n_ctx)
    xx = jnp.concatenate([x, ctx], axis=1)

    for l in range(depth):
        last = l == depth - 1
        tiles = lat_tiles if last else lat_tiles + 1
        w_k_p, w_vt_p = _prep_w_kv(w_kv_b[l])
        f_in, q, k, vt = _in_proj(xx, mod[l], g_mix[l][None], _prep_w_in(w_in[l]),
                                  g_q_a[l][None], _prep_w_q(w_q_b[l]),
                                  g_kv_a[l][None], w_k_p, w_vt_p, tabs, n_lat)
        att = _attention(q, k, vt, n_lat, tiles)
        w_f = w_fourier[l].astype(BF16)
        b_f = b_fourier[l][None]
        four = _fourier_latent(f_in[:, :n_lat], w_f, b_f)
        if not last:
            four = jnp.concatenate([four, _fourier_ctx(f_in, lat_tiles, w_f, b_f)], axis=1)
        w_o = w_out[l].astype(BF16)
        x1, h2 = _out_proj(xx, four, att, w_o[:FOURIER_WIDTH], w_o[FOURIER_WIDTH:], mod[l], g_ffn[l][None],
                           n_lat, tiles)
        xx = _conv_ffn(x1, h2, w_up[l].astype(BF16), w_dw[l], b_dw[l][None], w_down[l].astype(BF16),
                       mod[l], g_final[None], n_lat, last)
    return xx
```

```python
import functools
import math

import numpy as np
import jax
import jax.numpy as jnp
from jax import lax
from jax.experimental import pallas as pl
from jax.experimental.pallas import tpu as pltpu

F32 = jnp.float32
BF16 = jnp.bfloat16

D_MODEL = 1024
CTX_LEN = 256
GRID_W = 64
FOURIER_WIDTH = 256
N_FOURIER_HEADS = 4
FOURIER_HEAD_DIM = 64
V_HEAD_DIM = 64
QK_NOPE_DIM = 64
QK_ROPE_DIM = 32
QK_HEAD_DIM = 96
N_HEADS = 12
MLA_WIDTH = 768
Q_LORA_RANK = 384
KV_LORA_RANK = 128
D_FF = 2816
ROPE_THETA = 10000.0
NORM_EPS = 1e-6
SOFTMAX_SCALE = QK_HEAD_DIM ** -0.5
N_MOD = 6

LANES = 128
BF16_SUBLANES = 16
HEAD_PAD = LANES
ROW_TILE = 512
Q_TILE = 256
HEADS_PER_STEP = 4
KV_CHUNK = 512
VT_ROWS = V_HEAD_DIM + BF16_SUBLANES
M_INIT = -1e30
FF_CHUNK = 256
DFT_N2 = 64
DFT_K1_TILE = 8
VMEM_LIMIT = 56 * 1024 * 1024
LOG2E = math.log2(math.e)
HIGHEST = lax.Precision.HIGHEST


def _cparams(*sem):
    return pltpu.CompilerParams(dimension_semantics=sem, vmem_limit_bytes=VMEM_LIMIT)


def _rms(x, eps=NORM_EPS):
    return x * lax.rsqrt(jnp.mean(x * x, axis=-1, keepdims=True) + eps)


def _mod_kernel(c_ref, w_ref, b_ref, o_ref):
    c = c_ref[...]
    s = c / (1.0 + jnp.exp(-c))
    o_ref[...] = jnp.dot(s.astype(BF16), w_ref[...].astype(BF16),
                         preferred_element_type=F32) + b_ref[...]


def _modulation(cvec, w_ada, b_ada):
    depth, d, _ = w_ada.shape
    rows = cvec.shape[0]
    return pl.pallas_call(
        _mod_kernel,
        grid=(depth, N_MOD),
        in_specs=[
            pl.BlockSpec((rows, d), lambda l, j: (0, 0)),
            pl.BlockSpec((None, d, d), lambda l, j: (l, 0, j)),
            pl.BlockSpec((None, 1, d), lambda l, j: (l, 0, j)),
        ],
        out_specs=pl.BlockSpec((None, rows, d), lambda l, j: (l, 0, j)),
        out_shape=jax.ShapeDtypeStruct((depth, rows, N_MOD * d), F32),
        compiler_params=_cparams("arbitrary", "arbitrary"),
        name="adaln_mod",
    )(cvec, w_ada, b_ada.reshape(depth, 1, N_MOD * d))


def _in_proj_kernel(x_ref, mod_ref, g_ref, w_in_ref, gq_ref, wq_ref, gkv_ref, wk_ref, wvt_ref,
                    cq_ref, sq_ref, ck_ref, sk_ref,
                    f_ref, q_ref, k_ref, vt_ref, *, ctx_tile):
    x = x_ref[...]
    gain = g_ref[...] * (1.0 + mod_ref[1:2, :])
    h = _rms(x) * gain + mod_ref[0:1, :]
    p = jnp.dot(h.astype(BF16), w_in_ref[...], preferred_element_type=F32)
    f_ref[...] = p[:, :FOURIER_WIDTH]

    o_q = FOURIER_WIDTH
    o_kv = o_q + Q_LORA_RANK
    o_kr = o_kv + KV_LORA_RANK
    hw = N_HEADS * HEAD_PAD

    cq = _rms(p[:, o_q:o_kv]) * gq_ref[...]
    qq = jnp.dot(cq.astype(BF16), wq_ref[...], preferred_element_type=F32)
    cos_q = cq_ref[...]
    sin_q = sq_ref[...]
    for h_i in range(N_HEADS):
        lo = h_i * HEAD_PAD
        q_ref[h_i] = (qq[:, lo:lo + HEAD_PAD] * cos_q
                      + qq[:, hw + lo:hw + lo + HEAD_PAD] * sin_q).astype(BF16)

    ckv = (_rms(p[:, o_kv:o_kr]) * gkv_ref[...]).astype(BF16)
    kn = jnp.dot(ckv, wk_ref[...], preferred_element_type=F32)
    k_rope = p[:, o_kr:o_kr + HEAD_PAD] * ck_ref[...] + p[:, o_kr + HEAD_PAD:] * sk_ref[...]
    for h_i in range(N_HEADS):
        lo = h_i * HEAD_PAD
        k_ref[h_i] = (kn[:, lo:lo + HEAD_PAD] + k_rope).astype(BF16)
    vt = lax.dot_general(wvt_ref[...], ckv, (((1,), (1,)), ((), ())), preferred_element_type=F32)
    ones_row = (lax.broadcasted_iota(jnp.int32, (VT_ROWS, 1), 0) == V_HEAD_DIM).astype(F32)
    for h_i in range(N_HEADS):
        vt_ref[h_i] = (vt[h_i * VT_ROWS:(h_i + 1) * VT_ROWS] + ones_row).astype(BF16)

    @pl.when(pl.program_id(1) == ctx_tile)
    def _():
        k_ref[:, CTX_LEN:, :] = k_ref[:, :CTX_LEN, :]
        vt_ref[:, :, CTX_LEN:] = jnp.zeros((N_HEADS, VT_ROWS, ROW_TILE - CTX_LEN), BF16)


def _in_proj(xx, mod_l, g_mix, w_in_p, g_q, w_q_p, g_kv, w_k_p, w_vt_p, tabs, n_lat):
    b, nt, d = xx.shape
    tiles = nt // ROW_TILE
    ctx_tile = n_lat // ROW_TILE
    cos_q, sin_q, cos_k, sin_k = tabs
    const = lambda bi, t: (0, 0)
    wspec = lambda w: pl.BlockSpec(w.shape, const, pipeline_mode=pl.Buffered(1))
    tab_spec = pl.BlockSpec((ROW_TILE, HEAD_PAD), lambda bi, t: (t, 0))
    head_spec = pl.BlockSpec((None, N_HEADS, ROW_TILE, HEAD_PAD), lambda bi, t: (bi, 0, t, 0))
    head_shape = jax.ShapeDtypeStruct((b, N_HEADS, nt, HEAD_PAD), BF16)
    return pl.pallas_call(
        functools.partial(_in_proj_kernel, ctx_tile=ctx_tile),
        grid=(b, tiles),
        in_specs=[
            pl.BlockSpec((None, ROW_TILE, d), lambda bi, t: (bi, t, 0)),
            pl.BlockSpec((None, N_MOD, d), lambda bi, t: (jnp.where(t == ctx_tile, b, bi), 0, 0)),
            pl.BlockSpec((1, d), const),
            wspec(w_in_p),
            pl.BlockSpec((1, Q_LORA_RANK), const),
            wspec(w_q_p),
            pl.BlockSpec((1, KV_LORA_RANK), const),
            wspec(w_k_p),
            wspec(w_vt_p),
            tab_spec, tab_spec, tab_spec, tab_spec,
        ],
        out_specs=[
            pl.BlockSpec((None, ROW_TILE, FOURIER_WIDTH), lambda bi, t: (bi, t, 0)),
            head_spec, head_spec,
            pl.BlockSpec((None, N_HEADS, VT_ROWS, ROW_TILE), lambda bi, t: (bi, 0, 0, t)),
        ],
        out_shape=[
            jax.ShapeDtypeStruct((b, nt, FOURIER_WIDTH), F32),
            head_shape, head_shape,
            jax.ShapeDtypeStruct((b, N_HEADS, VT_ROWS, nt), BF16),
        ],
        compiler_params=_cparams("arbitrary", "arbitrary"),
        name="in_proj",
    )(xx, mod_l, g_mix, w_in_p, g_q, w_q_p, g_kv, w_k_p, w_vt_p, cos_q, sin_q, cos_k, sin_k)


def _attention_kernel(q_ref, k_ref, vt_ref, o_ref, s_scr, p_scr, m_scr, a_scr, acc_scr, *, n_lat):
    t = pl.program_id(2)
    n_all = k_ref.shape[1] // KV_CHUNK
    lat_chunks = n_lat // KV_CHUNK
    nt_dims = (((1,), (1,)), ((), ()))

    def qk(hh, start):
        return lax.dot_general(k_ref[hh, pl.ds(start, KV_CHUNK), :], q_ref[hh], nt_dims,
                               preferred_element_type=F32)

    def softmax(hh, s):
        m = m_scr[hh]
        m_new = jnp.maximum(m, jnp.max(s, axis=0, keepdims=True))
        m_scr[hh] = m_new
        a_scr[hh] = jnp.exp2(m - m_new)
        return jnp.exp2(s - m_new).astype(BF16)

    def pv(hh, p, start):
        acc_scr[hh] = a_scr[hh] * acc_scr[hh] + jnp.dot(vt_ref[hh, :, pl.ds(start, KV_CHUNK)], p,
                                                        preferred_element_type=F32)

    is_lat = t < n_lat // Q_TILE
    first = pl.multiple_of(jnp.where(is_lat, 0, lat_chunks) * KV_CHUNK, KV_CHUNK)
    for hh in range(HEADS_PER_STEP):
        m_scr[hh] = jnp.full((1, Q_TILE), M_INIT, F32)
        a_scr[hh] = jnp.ones((1, Q_TILE), F32)
        acc_scr[hh] = jnp.zeros((VT_ROWS, Q_TILE), F32)
        s_scr[hh] = qk(hh, first)

    @pl.when(is_lat)
    def _():
        for c in range(n_all - 1):
            s_cur = [s_scr[hh] for hh in range(HEADS_PER_STEP)]
            for hh in range(HEADS_PER_STEP):
                s_scr[hh] = qk(hh, (c + 1) * KV_CHUNK)
                if c > 0:
                    pv(hh, p_scr[hh], (c - 1) * KV_CHUNK)
            for hh in range(HEADS_PER_STEP):
                p_scr[hh] = softmax(hh, s_cur[hh])
        for hh in range(HEADS_PER_STEP):
            pv(hh, p_scr[hh], (n_all - 2) * KV_CHUNK)

    outs = []
    for hh in range(HEADS_PER_STEP):
        pv(hh, softmax(hh, s_scr[hh]), (n_all - 1) * KV_CHUNK)
        acc = acc_scr[hh]
        outs.append(acc[:V_HEAD_DIM] / acc[V_HEAD_DIM:V_HEAD_DIM + 1])
    o_ref[...] = jnp.concatenate(outs, axis=0).T.astype(o_ref.dtype)


def _attention(q, k, vt, n_lat, q_tiles):
    b, _, nkp, _ = k.shape
    return pl.pallas_call(
        functools.partial(_attention_kernel, n_lat=n_lat),
        grid=(b, N_HEADS // HEADS_PER_STEP, q_tiles),
        in_specs=[
            pl.BlockSpec((None, HEADS_PER_STEP, Q_TILE, HEAD_PAD), lambda bi, hp, t: (bi, hp, t, 0)),
            pl.BlockSpec((None, HEADS_PER_STEP, nkp, HEAD_PAD), lambda bi, hp, t: (bi, hp, 0, 0)),
            pl.BlockSpec((None, HEADS_PER_STEP, VT_ROWS, nkp), lambda bi, hp, t: (bi, hp, 0, 0)),
        ],
        out_specs=pl.BlockSpec((None, Q_TILE, HEADS_PER_STEP * V_HEAD_DIM), lambda bi, hp, t: (bi, t, hp)),
        out_shape=jax.ShapeDtypeStruct((b, q_tiles * Q_TILE, MLA_WIDTH), BF16),
        scratch_shapes=[
            pltpu.VMEM((HEADS_PER_STEP, KV_CHUNK, Q_TILE), F32),
            pltpu.VMEM((HEADS_PER_STEP, KV_CHUNK, Q_TILE), BF16),
            pltpu.VMEM((HEADS_PER_STEP, 1, Q_TILE), F32),
            pltpu.VMEM((HEADS_PER_STEP, 1, Q_TILE), F32),
            pltpu.VMEM((HEADS_PER_STEP, VT_ROWS, Q_TILE), F32),
        ],
        compiler_params=_cparams("arbitrary", "arbitrary", "arbitrary"),
        name="mla_attention",
    )(q, k, vt)


def _dft_mats(n):
    idx = np.arange(n)
    ang = 2.0 * np.pi * ((idx[:, None] * idx[None, :]) % n) / n
    return np.cos(ang), np.sin(ang)


def _channel_dft_mats():
    c, s = _dft_mats(FOURIER_HEAD_DIM)
    eye = np.eye(N_FOURIER_HEADS)
    return np.kron(eye, c), np.kron(eye, s)


def _dft_stage1_kernel(f1_ref, x_ref, y_ref):
    y_ref[...] = jnp.dot(f1_ref[...], x_ref[...], precision=HIGHEST, preferred_element_type=F32)


def _dft_stage2_kernel(yr_ref, yi_ref, tc_ref, ts_ref, f2_ref, cb_ref, sb_ref, wf_ref, bf_ref, o_ref):
    xr, xi = [], []
    for j in range(DFT_K1_TILE):
        yr, yi = yr_ref[j], yi_ref[j]
        tc = jnp.concatenate([tc_ref[j]] * (FOURIER_WIDTH // LANES), axis=-1)
        ts = jnp.concatenate([ts_ref[j]] * (FOURIER_WIDTH // LANES), axis=-1)
        z = jnp.concatenate([yr * tc + yi * ts, yi * tc - yr * ts], axis=0)
        xx = jnp.dot(f2_ref[...], z, precision=HIGHEST, preferred_element_type=F32)
        xr.append(xx[:DFT_N2])
        xi.append(xx[DFT_N2:])
    xr = jnp.concatenate(xr, axis=0)
    xi = jnp.concatenate(xi, axis=0)
    four = _channel_mix(xr, xi, cb_ref, sb_ref, wf_ref, bf_ref)
    for j in range(DFT_K1_TILE):
        o_ref[:, j * FOURIER_WIDTH:(j + 1) * FOURIER_WIDTH] = four[j * DFT_N2:(j + 1) * DFT_N2]


def _channel_mix(xr, xi, cb_ref, sb_ref, wf_ref, bf_ref):
    f = (jnp.dot(xr, cb_ref[...], precision=HIGHEST, preferred_element_type=F32)
         + jnp.dot(xi, sb_ref[...], precision=HIGHEST, preferred_element_type=F32))
    return jnp.dot(f.astype(BF16), wf_ref[...], preferred_element_type=F32) + bf_ref[...]


def _dft_ctx_kernel(u_ref, fn_ref, cb_ref, sb_ref, wf_ref, bf_ref, o_ref):
    n = u_ref.shape[0]
    xx = jnp.dot(fn_ref[...], u_ref[...], precision=HIGHEST, preferred_element_type=F32)
    o_ref[...] = _channel_mix(xx[:n], xx[n:], cb_ref, sb_ref, wf_ref, bf_ref)


def _fourier_latent(f_lat, w_f, b_f):
    b, n, c = f_lat.shape
    n1 = n // DFT_N2
    scale = 1.0 / math.sqrt(n * FOURIER_HEAD_DIM)
    c1, s1 = _dft_mats(n1)
    f1 = jnp.asarray(np.concatenate([c1, -s1], axis=0) * scale, F32)
    c2, s2 = _dft_mats(DFT_N2)
    f2 = jnp.asarray(np.block([[c2, s2], [-s2, c2]]), F32)
    cb, sb = (jnp.asarray(m, F32) for m in _channel_dft_mats())
    k1 = lax.broadcasted_iota(jnp.int32, (n1, DFT_N2, LANES), 0)
    n2 = lax.broadcasted_iota(jnp.int32, (n1, DFT_N2, LANES), 1)
    ang = ((k1 * n2) % n).astype(F32) * (2.0 * math.pi / n)
    tw_c, tw_s = jnp.cos(ang), jnp.sin(ang)

    width = DFT_N2 * c
    col_tile = min(width, 2048)
    y = pl.pallas_call(
        _dft_stage1_kernel,
        grid=(b, width // col_tile),
        in_specs=[
            pl.BlockSpec((2 * n1, n1), lambda bi, j: (0, 0)),
            pl.BlockSpec((None, n1, col_tile), lambda bi, j: (bi, 0, j)),
        ],
        out_specs=pl.BlockSpec((None, 2 * n1, col_tile), lambda bi, j: (bi, 0, j)),
        out_shape=jax.ShapeDtypeStruct((b, 2 * n1, width), F32),
        compiler_params=_cparams("arbitrary", "arbitrary"),
        name="dft_stage1",
    )(f1, f_lat.reshape(b, n1, width))

    y = y.reshape(b, 2, n1, DFT_N2, c)
    kt = DFT_K1_TILE
    const = lambda bi, j: (0, 0)
    tw_spec = pl.BlockSpec((kt, DFT_N2, LANES), lambda bi, j: (j, 0, 0))
    out = pl.pallas_call(
        _dft_stage2_kernel,
        grid=(b, n1 // kt),
        in_specs=[
            pl.BlockSpec((None, None, kt, DFT_N2, c), lambda bi, j: (bi, 0, j, 0, 0)),
            pl.BlockSpec((None, None, kt, DFT_N2, c), lambda bi, j: (bi, 1, j, 0, 0)),
            tw_spec, tw_spec,
            pl.BlockSpec(f2.shape, const),
            pl.BlockSpec(cb.shape, const),
            pl.BlockSpec(sb.shape, const),
            pl.BlockSpec(w_f.shape, const),
            pl.BlockSpec((1, c), const),
        ],
        out_specs=pl.BlockSpec((None, DFT_N2, kt * c), lambda bi, j: (bi, 0, j)),
        out_shape=jax.ShapeDtypeStruct((b, DFT_N2, n1 * c), F32),
        compiler_params=_cparams("arbitrary", "arbitrary"),
        name="dft_stage2",
    )(y, y, tw_c, tw_s, f2, cb, sb, w_f, b_f)
    return out.reshape(b, n, c)


def _fourier_ctx(f_all, ctx_tile, w_f, b_f):
    b, _, c = f_all.shape
    n = CTX_LEN
    scale = 1.0 / math.sqrt(n * FOURIER_HEAD_DIM)
    cn, sn = _dft_mats(n)
    fn = jnp.asarray(np.concatenate([cn, -sn], axis=0) * scale, F32)
    cb, sb = (jnp.asarray(m, F32) for m in _channel_dft_mats())
    const = lambda bi: (0, 0)
    return pl.pallas_call(
        _dft_ctx_kernel,
        grid=(b,),
        in_specs=[
            pl.BlockSpec((None, n, c), lambda bi: (bi, ctx_tile, 0)),
            pl.BlockSpec(fn.shape, const),
            pl.BlockSpec(cb.shape, const),
            pl.BlockSpec(sb.shape, const),
            pl.BlockSpec(w_f.shape, const),
            pl.BlockSpec((1, c), const),
        ],
        out_specs=pl.BlockSpec((None, n, c), lambda bi: (bi, 0, 0)),
        out_shape=jax.ShapeDtypeStruct((b, n, c), F32),
        compiler_params=_cparams("arbitrary"),
        name="dft_ctx",
    )(f_all, fn, cb, sb, w_f, b_f)


def _out_proj_kernel(x_ref, four_ref, att_ref, wf_ref, wa_ref, mod_ref, g_ref, x1_ref, h2_ref):
    mix = (jnp.dot(four_ref[...].astype(BF16), wf_ref[...], preferred_element_type=F32)
           + jnp.dot(att_ref[...], wa_ref[...], preferred_element_type=F32))
    x1 = x_ref[...] + mod_ref[2:3, :] * mix
    x1_ref[...] = x1
    gain = g_ref[...] * (1.0 + mod_ref[4:5, :])
    h2_ref[...] = (_rms(x1) * gain + mod_ref[3:4, :]).astype(BF16)


def _out_proj(xx, four, att, w_out_f, w_out_a, mod_l, g_ffn, n_lat, tiles):
    b, _, d = xx.shape
    ctx_tile = n_lat // ROW_TILE
    const = lambda bi, t: (0, 0)
    row = lambda bi, t: (bi, t, 0)
    rows = tiles * ROW_TILE
    return pl.pallas_call(
        _out_proj_kernel,
        grid=(b, tiles),
        in_specs=[
            pl.BlockSpec((None, ROW_TILE, d), row),
            pl.BlockSpec((None, ROW_TILE, FOURIER_WIDTH), row),
            pl.BlockSpec((None, ROW_TILE, MLA_WIDTH), row),
            pl.BlockSpec(w_out_f.shape, const, pipeline_mode=pl.Buffered(1)),
            pl.BlockSpec(w_out_a.shape, const, pipeline_mode=pl.Buffered(1)),
            pl.BlockSpec((None, N_MOD, d), lambda bi, t: (jnp.where(t == ctx_tile, b, bi), 0, 0)),
            pl.BlockSpec((1, d), const),
        ],
        out_specs=[pl.BlockSpec((None, ROW_TILE, d), row), pl.BlockSpec((None, ROW_TILE, d), row)],
        out_shape=[jax.ShapeDtypeStruct((b, rows, d), F32), jax.ShapeDtypeStruct((b, rows, d), BF16)],
        compiler_params=_cparams("arbitrary", "arbitrary"),
        name="out_proj",
    )(xx, four, att, w_out_f, w_out_a, mod_l, g_ffn)


def _conv_ffn_kernel(h_ref, hp_ref, hn_ref, x_ref, wup_ref, wdw_ref, bdw_ref, wdn_ref, mod_ref, gf_ref,
                     o_ref, *, n_lat, final):
    t = pl.program_id(1)
    T = ROW_TILE
    lat_tiles = n_lat // T
    is_ctx = t == lat_tiles
    has_prev = jnp.logical_and(t != 0, t != lat_tiles)
    has_next = t < lat_tiles - 1
    h = h_ref[...]
    hrow = lax.broadcasted_iota(jnp.int32, (BF16_SUBLANES, 1), 0)
    halo = (jnp.where(jnp.logical_and(hrow == BF16_SUBLANES - 1, has_prev), hp_ref[...], 0)
            + jnp.where(jnp.logical_and(hrow == 0, has_next), hn_ref[...], 0)).astype(BF16)
    r8 = lax.broadcasted_iota(jnp.int32, (8, 1), 0)
    ctx_end = CTX_LEN

    def up(col):
        w = wup_ref[:, col:col + FF_CHUNK]
        return jnp.dot(h, w, preferred_element_type=F32), jnp.dot(halo, w, preferred_element_type=F32)

    def conv(col, u, uh):
        u_prev = uh[BF16_SUBLANES - 1:, :]
        u_next = uh[:1, :]
        below = pltpu.roll(u, 1, axis=0)
        above = pltpu.roll(u, T - 1, axis=0)
        below = jnp.concatenate([jnp.where(r8 == 0, u_prev, below[:8]), below[8:]], axis=0)
        last = jnp.where(r8 == 7, u_next, above[T - 8:])
        mid = jnp.where(jnp.logical_and(r8 == 7, is_ctx), 0.0, above[ctx_end - 8:ctx_end])
        above = jnp.concatenate([above[:ctx_end - 8], mid, above[ctx_end:T - 8], last], axis=0)
        wd = wdw_ref[:, col:col + FF_CHUNK]
        return below * wd[0:1] + u * wd[1:2] + above * wd[2:3] + bdw_ref[:, col:col + FF_CHUNK]

    def vec(ci, raw):
        (ug, uhg), (uv, uhv) = raw
        gate = conv(ci * FF_CHUNK, ug, uhg)
        val = conv(D_FF + ci * FF_CHUNK, uv, uhv)
        return (gate / (1.0 + jnp.exp(-gate)) * val).astype(BF16)

    def upc(ci):
        return up(ci * FF_CHUNK), up(D_FF + ci * FF_CHUNK)

    n_chunks = D_FF // FF_CHUNK
    acc = None
    raw = upc(0)
    act_prev = None
    for ci in range(n_chunks):
        raw_next = upc(ci + 1) if ci + 1 < n_chunks else None
        if act_prev is not None:
            d = jnp.dot(act_prev, wdn_ref[(ci - 1) * FF_CHUNK:ci * FF_CHUNK, :], preferred_element_type=F32)
            acc = d if acc is None else acc + d
        act_prev = vec(ci, raw)
        raw = raw_next
    acc = acc + jnp.dot(act_prev, wdn_ref[(n_chunks - 1) * FF_CHUNK:, :], preferred_element_type=F32)
    out = x_ref[...] + mod_ref[5:6, :] * acc
    if final:
        out = _rms(out) * gf_ref[...]
    o_ref[...] = out


def _conv_ffn(x1, h2, w_up, w_dw, b_dw, w_down, mod_l, g_final, n_lat, final):
    b, rows, d = x1.shape
    T = ROW_TILE
    tiles = rows // T
    ctx_tile = n_lat // T
    halo = BF16_SUBLANES
    per_tile = T // halo
    last_halo = rows // halo - 1
    const = lambda bi, t: (0, 0)
    row = lambda bi, t: (bi, t, 0)
    wspec = lambda shape: pl.BlockSpec(shape, const, pipeline_mode=pl.Buffered(1))
    return pl.pallas_call(
        functools.partial(_conv_ffn_kernel, n_lat=n_lat, final=final),
        grid=(b, tiles),
        in_specs=[
            pl.BlockSpec((None, T, d), row),
            pl.BlockSpec((None, halo, d), lambda bi, t: (bi, jnp.maximum(t * per_tile - 1, 0), 0)),
            pl.BlockSpec((None, halo, d), lambda bi, t: (bi, jnp.minimum((t + 1) * per_tile, last_halo), 0)),
            pl.BlockSpec((None, T, d), row),
            wspec(w_up.shape),
            pl.BlockSpec(w_dw.shape, const),
            pl.BlockSpec((1, 2 * D_FF), const),
            wspec(w_down.shape),
            pl.BlockSpec((None, N_MOD, d), lambda bi, t: (jnp.where(t == ctx_tile, b, bi), 0, 0)),
            pl.BlockSpec((1, d), const),
        ],
        out_specs=pl.BlockSpec((None, T, d), row),
        out_shape=jax.ShapeDtypeStruct((b, rows, d), F32),
        compiler_params=_cparams("arbitrary", "arbitrary"),
        name="conv_ffn",
    )(h2, h2, h2, x1, w_up, w_dw, b_dw, w_down, mod_l, g_final)


def _prep_w_in(w_in):
    d = w_in.shape[0]
    o_kr = FOURIER_WIDTH + Q_LORA_RANK + KV_LORA_RANK
    half = QK_ROPE_DIM // 2
    x1 = w_in[:, o_kr:o_kr + half]
    x2 = w_in[:, o_kr + half:o_kr + 2 * half]
    z_lo = jnp.zeros((d, QK_NOPE_DIM), w_in.dtype)
    z_hi = jnp.zeros((d, HEAD_PAD - QK_HEAD_DIM), w_in.dtype)
    main = jnp.concatenate([z_lo, x1, x2, z_hi], axis=1)
    swap = jnp.concatenate([z_lo, -x2, x1, z_hi], axis=1)
    return jnp.concatenate([w_in[:, :o_kr], main, swap], axis=1).astype(BF16)


def _prep_w_q(w_q_b):
    r = w_q_b.shape[0]
    half = QK_ROPE_DIM // 2
    w = w_q_b.reshape(r, N_HEADS, QK_HEAD_DIM)
    nope = w[..., :QK_NOPE_DIM]
    x1 = w[..., QK_NOPE_DIM:QK_NOPE_DIM + half]
    x2 = w[..., QK_NOPE_DIM + half:]
    z_hi = jnp.zeros((r, N_HEADS, HEAD_PAD - QK_HEAD_DIM), w.dtype)
    main = jnp.concatenate([nope, x1, x2, z_hi], axis=-1).reshape(r, N_HEADS * HEAD_PAD)
    swap = jnp.concatenate([jnp.zeros_like(nope), -x2, x1, z_hi], axis=-1).reshape(r, N_HEADS * HEAD_PAD)
    return jnp.concatenate([main, swap], axis=1).astype(BF16)


def _prep_w_kv(w_kv_b):
    r = w_kv_b.shape[0]
    w = w_kv_b.reshape(r, N_HEADS, QK_NOPE_DIM + V_HEAD_DIM)
    z = jnp.zeros((r, N_HEADS, HEAD_PAD - QK_NOPE_DIM), w.dtype)
    k = jnp.concatenate([w[..., :QK_NOPE_DIM], z], axis=-1).reshape(r, N_HEADS * HEAD_PAD)
    zv = jnp.zeros((r, N_HEADS, VT_ROWS - V_HEAD_DIM), w.dtype)
    v = jnp.concatenate([w[..., QK_NOPE_DIM:], zv], axis=-1).reshape(r, N_HEADS * VT_ROWS)
    return k.astype(BF16), v.T.astype(BF16)


def _rope_tables(n_lat, n_rest):
    rows = n_lat // GRID_W
    row_ids = jnp.broadcast_to(jnp.arange(rows)[:, None], (rows, GRID_W)).reshape(-1).astype(F32)
    col_ids = jnp.broadcast_to(jnp.arange(GRID_W)[None, :], (rows, GRID_W)).reshape(-1).astype(F32)
    n_freq = QK_ROPE_DIM // 4
    inv_freq = ROPE_THETA ** (-jnp.arange(n_freq, dtype=F32) / n_freq)
    ang = jnp.concatenate([row_ids[:, None] * inv_freq, col_ids[:, None] * inv_freq], axis=-1)
    cos, sin = jnp.cos(ang), jnp.sin(ang)
    cos = jnp.concatenate([cos, jnp.ones((n_rest, cos.shape[1]), F32)], axis=0)
    sin = jnp.concatenate([sin, jnp.zeros((n_rest, sin.shape[1]), F32)], axis=0)
    nt = n_lat + n_rest
    ones_lo = jnp.ones((nt, QK_NOPE_DIM), F32)
    ones_hi = jnp.ones((nt, HEAD_PAD - QK_HEAD_DIM), F32)
    cos_t = jnp.concatenate([ones_lo, cos, cos, ones_hi], axis=1)
    sin_t = jnp.concatenate([0.0 * ones_lo, sin, sin, 0.0 * ones_hi], axis=1)
    qs = SOFTMAX_SCALE * LOG2E
    return cos_t * qs, sin_t * qs, cos_t, sin_t


def kernel(x, c, ctx, c_ctx, w_ada, b_ada, g_mix, w_in, w_fourier, b_fourier, g_q_a, w_q_b, g_kv_a, w_kv_b,
           w_out, g_ffn, w_up, w_dw, b_dw, w_down, g_final):
    b, n_lat, d = x.shape
    n_ctx = ctx.shape[1]
    depth = w_ada.shape[0]
    assert n_ctx == CTX_LEN == Q_TILE and 2 * n_ctx == ROW_TILE == KV_CHUNK
    assert n_lat % ROW_TILE == 0 and n_lat % (DFT_N2 * DFT_K1_TILE) == 0
    lat_tiles = n_lat // ROW_TILE
    n_pad = ROW_TILE - n_ctx

    pad = (-(b + 1)) % 8
    cvec = jnp.concatenate([c, c_ctx[None, :], jnp.zeros((pad, d), F32)], axis=0)
    mod = _modulation(cvec, w_ada, b_ada).reshape(depth, b + 1 + pad, N_MOD, d)

    tabs = _rope_tables(n_lat, n_ctx + n_pad)
    xx = jnp.concatenate([x, ctx, jnp.zeros((b, n_pad, d), F32)], axis=1)

    for l in range(depth):
        last = l == depth - 1
        tiles = lat_tiles if last else lat_tiles + 1
        w_k_p, w_vt_p = _prep_w_kv(w_kv_b[l])
        f_in, q, k, vt = _in_proj(xx, mod[l], g_mix[l][None], _prep_w_in(w_in[l]),
                                  g_q_a[l][None], _prep_w_q(w_q_b[l]),
                                  g_kv_a[l][None], w_k_p, w_vt_p, tabs, n_lat)
        att = _attention(q, k, vt, n_lat, tiles * (ROW_TILE // Q_TILE))
        w_f = w_fourier[l].astype(BF16)
        b_f = b_fourier[l][None]
        four = _fourier_latent(f_in[:, :n_lat], w_f, b_f)
        if not last:
            four_c = _fourier_ctx(f_in, n_lat // CTX_LEN, w_f, b_f)
            four = jnp.concatenate([four, four_c, jnp.zeros((b, n_pad, FOURIER_WIDTH), F32)], axis=1)
        w_o = w_out[l].astype(BF16)
        x1, h2 = _out_proj(xx, four, att, w_o[:FOURIER_WIDTH], w_o[FOURIER_WIDTH:], mod[l], g_ffn[l][None],
                           n_lat, tiles)
        xx = _conv_ffn(x1, h2, w_up[l].astype(BF16), w_dw[l], b_dw[l][None], w_down[l].astype(BF16),
                       mod[l], g_final[None], n_lat, last)
    return xx
```

```python
import functools
import math

import numpy as np
import jax
import jax.numpy as jnp
from jax import lax
from jax.experimental import pallas as pl
from jax.experimental.pallas import tpu as pltpu

F32 = jnp.float32
BF16 = jnp.bfloat16

D_MODEL = 1024
CTX_LEN = 256
GRID_W = 64
FOURIER_WIDTH = 256
N_FOURIER_HEADS = 4
FOURIER_HEAD_DIM = 64
V_HEAD_DIM = 64
QK_NOPE_DIM = 64
QK_ROPE_DIM = 32
QK_HEAD_DIM = 96
N_HEADS = 12
MLA_WIDTH = 768
Q_LORA_RANK = 384
KV_LORA_RANK = 128
D_FF = 2816
ROPE_THETA = 10000.0
NORM_EPS = 1e-6
SOFTMAX_SCALE = QK_HEAD_DIM ** -0.5
N_MOD = 6

LANES = 128
BF16_SUBLANES = 16
HEAD_PAD = LANES
ROW_TILE = 512
Q_TILE = 256
HEADS_PER_STEP = 4
KV_CHUNK = 512
VT_ROWS = V_HEAD_DIM + BF16_SUBLANES
M_INIT = -1e30
FF_CHUNK = 256
DFT_N2 = 64
DFT_K1_TILE = 8
VMEM_LIMIT = 56 * 1024 * 1024
LOG2E = math.log2(math.e)
HIGHEST = lax.Precision.HIGHEST


def _cparams(*sem):
    return pltpu.CompilerParams(dimension_semantics=sem, vmem_limit_bytes=VMEM_LIMIT)


def _rms(x, eps=NORM_EPS):
    return x * lax.rsqrt(jnp.mean(x * x, axis=-1, keepdims=True) + eps)


def _mod_kernel(c_ref, w_ref, b_ref, o_ref):
    c = c_ref[...]
    s = c / (1.0 + jnp.exp(-c))
    o_ref[...] = jnp.dot(s.astype(BF16), w_ref[...].astype(BF16),
                         preferred_element_type=F32) + b_ref[...]


def _modulation(cvec, w_ada, b_ada):
    depth, d, _ = w_ada.shape
    rows = cvec.shape[0]
    return pl.pallas_call(
        _mod_kernel,
        grid=(depth, N_MOD),
        in_specs=[
            pl.BlockSpec((rows, d), lambda l, j: (0, 0)),
            pl.BlockSpec((None, d, d), lambda l, j: (l, 0, j)),
            pl.BlockSpec((None, 1, d), lambda l, j: (l, 0, j)),
        ],
        out_specs=pl.BlockSpec((None, rows, d), lambda l, j: (l, 0, j)),
        out_shape=jax.ShapeDtypeStruct((depth, rows, N_MOD * d), F32),
        compiler_params=_cparams("arbitrary", "arbitrary"),
        name="adaln_mod",
    )(cvec, w_ada, b_ada.reshape(depth, 1, N_MOD * d))


def _in_proj_kernel(x_ref, mod_ref, g_ref, w_in_ref, gq_ref, wq_ref, gkv_ref, wk_ref, wvt_ref,
                    cq_ref, sq_ref, ck_ref, sk_ref,
                    f_ref, q_ref, k_ref, vt_ref, *, ctx_tile):
    x = x_ref[...]
    gain = g_ref[...] * (1.0 + mod_ref[1:2, :])
    h = _rms(x) * gain + mod_ref[0:1, :]
    p = jnp.dot(h.astype(BF16), w_in_ref[...], preferred_element_type=F32)
    f_ref[...] = p[:, :FOURIER_WIDTH]

    o_q = FOURIER_WIDTH
    o_kv = o_q + Q_LORA_RANK
    o_kr = o_kv + KV_LORA_RANK
    hw = N_HEADS * HEAD_PAD

    cq = _rms(p[:, o_q:o_kv]) * gq_ref[...]
    qq = jnp.dot(cq.astype(BF16), wq_ref[...], preferred_element_type=F32)
    cos_q = cq_ref[...]
    sin_q = sq_ref[...]
    for h_i in range(N_HEADS):
        lo = h_i * HEAD_PAD
        q_ref[h_i] = (qq[:, lo:lo + HEAD_PAD] * cos_q
                      + qq[:, hw + lo:hw + lo + HEAD_PAD] * sin_q).astype(BF16)

    ckv = (_rms(p[:, o_kv:o_kr]) * gkv_ref[...]).astype(BF16)
    kn = jnp.dot(ckv, wk_ref[...], preferred_element_type=F32)
    k_rope = p[:, o_kr:o_kr + HEAD_PAD] * ck_ref[...] + p[:, o_kr + HEAD_PAD:] * sk_ref[...]
    for h_i in range(N_HEADS):
        lo = h_i * HEAD_PAD
        k_ref[h_i] = (kn[:, lo:lo + HEAD_PAD] + k_rope).astype(BF16)
    vt = lax.dot_general(wvt_ref[...], ckv, (((1,), (1,)), ((), ())), preferred_element_type=F32)
    ones_row = (lax.broadcasted_iota(jnp.int32, (VT_ROWS, 1), 0) == V_HEAD_DIM).astype(F32)
    for h_i in range(N_HEADS):
        vt_ref[h_i] = (vt[h_i * VT_ROWS:(h_i + 1) * VT_ROWS] + ones_row).astype(BF16)

    @pl.when(pl.program_id(1) == ctx_tile)
    def _():
        k_ref[:, CTX_LEN:, :] = k_ref[:, :CTX_LEN, :]
        vt_ref[:, :, CTX_LEN:] = jnp.zeros((N_HEADS, VT_ROWS, ROW_TILE - CTX_LEN), BF16)


def _in_proj(xx, mod_l, g_mix, w_in_p, g_q, w_q_p, g_kv, w_k_p, w_vt_p, tabs, n_lat):
    b, nt, d = xx.shape
    tiles = nt // ROW_TILE
    ctx_tile = n_lat // ROW_TILE
    cos_q, sin_q, cos_k, sin_k = tabs
    const = lambda bi, t: (0, 0)
    wspec = lambda w: pl.BlockSpec(w.shape, const, pipeline_mode=pl.Buffered(1))
    tab_spec = pl.BlockSpec((ROW_TILE, HEAD_PAD), lambda bi, t: (t, 0))
    head_spec = pl.BlockSpec((None, N_HEADS, ROW_TILE, HEAD_PAD), lambda bi, t: (bi, 0, t, 0))
    head_shape = jax.ShapeDtypeStruct((b, N_HEADS, nt, HEAD_PAD), BF16)
    return pl.pallas_call(
        functools.partial(_in_proj_kernel, ctx_tile=ctx_tile),
        grid=(b, tiles),
        in_specs=[
            pl.BlockSpec((None, ROW_TILE, d), lambda bi, t: (bi, t, 0)),
            pl.BlockSpec((None, N_MOD, d), lambda bi, t: (jnp.where(t == ctx_tile, b, bi), 0, 0)),
            pl.BlockSpec((1, d), const),
            wspec(w_in_p),
            pl.BlockSpec((1, Q_LORA_RANK), const),
            wspec(w_q_p),
            pl.BlockSpec((1, KV_LORA_RANK), const),
            wspec(w_k_p),
            wspec(w_vt_p),
            tab_spec, tab_spec, tab_spec, tab_spec,
        ],
        out_specs=[
            pl.BlockSpec((None, ROW_TILE, FOURIER_WIDTH), lambda bi, t: (bi, t, 0)),
            head_spec, head_spec,
            pl.BlockSpec((None, N_HEADS, VT_ROWS, ROW_TILE), lambda bi, t: (bi, 0, 0, t)),
        ],
        out_shape=[
            jax.ShapeDtypeStruct((b, nt, FOURIER_WIDTH), F32),
            head_shape, head_shape,
            jax.ShapeDtypeStruct((b, N_HEADS, VT_ROWS, nt), BF16),
        ],
        compiler_params=_cparams("arbitrary", "arbitrary"),
        name="in_proj",
    )(xx, mod_l, g_mix, w_in_p, g_q, w_q_p, g_kv, w_k_p, w_vt_p, cos_q, sin_q, cos_k, sin_k)


def _attention_kernel(q_ref, k_ref, vt_ref, o_ref, s_scr, p_scr, m_scr, a_scr, acc_scr, *, n_lat):
    t = pl.program_id(2)
    n_all = k_ref.shape[1] // KV_CHUNK
    lat_chunks = n_lat // KV_CHUNK
    nt_dims = (((1,), (1,)), ((), ()))

    def qk(hh, start):
        return lax.dot_general(k_ref[hh, pl.ds(start, KV_CHUNK), :], q_ref[hh], nt_dims,
                               preferred_element_type=F32)

    def softmax(hh, s):
        m = m_scr[hh]
        m_new = jnp.maximum(m, jnp.max(s, axis=0, keepdims=True))
        m_scr[hh] = m_new
        a_scr[hh] = jnp.exp2(m - m_new)
        return jnp.exp2(s - m_new).astype(BF16)

    def pv(hh, p, start):
        acc_scr[hh] = a_scr[hh] * acc_scr[hh] + jnp.dot(vt_ref[hh, :, pl.ds(start, KV_CHUNK)], p,
                                                        preferred_element_type=F32)

    is_lat = t < n_lat // Q_TILE
    first = pl.multiple_of(jnp.where(is_lat, 0, lat_chunks) * KV_CHUNK, KV_CHUNK)
    for hh in range(HEADS_PER_STEP):
        m_scr[hh] = jnp.full((1, Q_TILE), M_INIT, F32)
        a_scr[hh] = jnp.ones((1, Q_TILE), F32)
        acc_scr[hh] = jnp.zeros((VT_ROWS, Q_TILE), F32)
        s_scr[hh] = qk(hh, first)

    @pl.when(is_lat)
    def _():
        for c in range(n_all - 1):
            s_cur = [s_scr[hh] for hh in range(HEADS_PER_STEP)]
            for hh in range(HEADS_PER_STEP):
                s_scr[hh] = qk(hh, (c + 1) * KV_CHUNK)
                if c > 0:
                    pv(hh, p_scr[hh], (c - 1) * KV_CHUNK)
            for hh in range(HEADS_PER_STEP):
                p_scr[hh] = softmax(hh, s_cur[hh])
        for hh in range(HEADS_PER_STEP):
            pv(hh, p_scr[hh], (n_all - 2) * KV_CHUNK)

    outs = []
    for hh in range(HEADS_PER_STEP):
        pv(hh, softmax(hh, s_scr[hh]), (n_all - 1) * KV_CHUNK)
        acc = acc_scr[hh]
        outs.append(acc[:V_HEAD_DIM] / acc[V_HEAD_DIM:V_HEAD_DIM + 1])
    o_ref[...] = jnp.concatenate(outs, axis=0).T.astype(o_ref.dtype)


def _attention(q, k, vt, n_lat, q_tiles):
    b, _, nkp, _ = k.shape
    return pl.pallas_call(
        functools.partial(_attention_kernel, n_lat=n_lat),
        grid=(b, N_HEADS // HEADS_PER_STEP, q_tiles),
        in_specs=[
            pl.BlockSpec((None, HEADS_PER_STEP, Q_TILE, HEAD_PAD), lambda bi, hp, t: (bi, hp, t, 0)),
            pl.BlockSpec((None, HEADS_PER_STEP, nkp, HEAD_PAD), lambda bi, hp, t: (bi, hp, 0, 0)),
            pl.BlockSpec((None, HEADS_PER_STEP, VT_ROWS, nkp), lambda bi, hp, t: (bi, hp, 0, 0)),
        ],
        out_specs=pl.BlockSpec((None, Q_TILE, HEADS_PER_STEP * V_HEAD_DIM), lambda bi, hp, t: (bi, t, hp)),
        out_shape=jax.ShapeDtypeStruct((b, q_tiles * Q_TILE, MLA_WIDTH), BF16),
        scratch_shapes=[
            pltpu.VMEM((HEADS_PER_STEP, KV_CHUNK, Q_TILE), F32),
            pltpu.VMEM((HEADS_PER_STEP, KV_CHUNK, Q_TILE), BF16),
            pltpu.VMEM((HEADS_PER_STEP, 1, Q_TILE), F32),
            pltpu.VMEM((HEADS_PER_STEP, 1, Q_TILE), F32),
            pltpu.VMEM((HEADS_PER_STEP, VT_ROWS, Q_TILE), F32),
        ],
        compiler_params=_cparams("arbitrary", "arbitrary", "arbitrary"),
        name="mla_attention",
    )(q, k, vt)


def _dft_mats(n):
    idx = np.arange(n)
    ang = 2.0 * np.pi * ((idx[:, None] * idx[None, :]) % n) / n
    return np.cos(ang), np.sin(ang)


def _channel_dft_mats():
    c, s = _dft_mats(FOURIER_HEAD_DIM)
    eye = np.eye(N_FOURIER_HEADS)
    return np.kron(eye, c), np.kron(eye, s)


def _dft_stage1_kernel(f1_ref, x_ref, y_ref):
    n1 = x_ref.shape[0]
    for jj in range(x_ref.shape[1]):
        y = jnp.dot(f1_ref[...], x_ref[:, jj, :], precision=HIGHEST, preferred_element_type=F32)
        y_ref[0, :, jj, :] = y[:n1]
        y_ref[1, :, jj, :] = y[n1:]


def _dft_stage2_kernel(yr_ref, yi_ref, tc_ref, ts_ref, f2_ref, cb_ref, sb_ref, wf_ref, bf_ref, o_ref):
    xr, xi = [], []
    for j in range(DFT_K1_TILE):
        yr, yi = yr_ref[j], yi_ref[j]
        tc = jnp.concatenate([tc_ref[j]] * (FOURIER_WIDTH // LANES), axis=-1)
        ts = jnp.concatenate([ts_ref[j]] * (FOURIER_WIDTH // LANES), axis=-1)
        z = jnp.concatenate([yr * tc + yi * ts, yi * tc - yr * ts], axis=0)
        xx = jnp.dot(f2_ref[...], z, precision=HIGHEST, preferred_element_type=F32)
        xr.append(xx[:DFT_N2])
        xi.append(xx[DFT_N2:])
    xr = jnp.concatenate(xr, axis=0)
    xi = jnp.concatenate(xi, axis=0)
    four = _channel_mix(xr, xi, cb_ref, sb_ref, wf_ref, bf_ref)
    for j in range(DFT_K1_TILE):
        o_ref[:, j, :] = four[j * DFT_N2:(j + 1) * DFT_N2]


def _channel_mix(xr, xi, cb_ref, sb_ref, wf_ref, bf_ref):
    f = (jnp.dot(xr, cb_ref[...], precision=HIGHEST, preferred_element_type=F32)
         + jnp.dot(xi, sb_ref[...], precision=HIGHEST, preferred_element_type=F32))
    return jnp.dot(f.astype(BF16), wf_ref[...], preferred_element_type=F32) + bf_ref[...]


def _dft_ctx_kernel(u_ref, fn_ref, cb_ref, sb_ref, wf_ref, bf_ref, o_ref):
    n = u_ref.shape[0]
    xx = jnp.dot(fn_ref[...], u_ref[...], precision=HIGHEST, preferred_element_type=F32)
    o_ref[...] = _channel_mix(xx[:n], xx[n:], cb_ref, sb_ref, wf_ref, bf_ref)


def _fourier_latent(f_all, n, w_f, b_f):
    b, ntot, c = f_all.shape
    n1 = n // DFT_N2
    scale = 1.0 / math.sqrt(n * FOURIER_HEAD_DIM)
    c1, s1 = _dft_mats(n1)
    f1 = jnp.asarray(np.concatenate([c1, -s1], axis=0) * scale, F32)
    c2, s2 = _dft_mats(DFT_N2)
    f2 = jnp.asarray(np.block([[c2, s2], [-s2, c2]]), F32)
    cb, sb = (jnp.asarray(m, F32) for m in _channel_dft_mats())
    k1 = lax.broadcasted_iota(jnp.int32, (n1, DFT_N2, LANES), 0)
    n2 = lax.broadcasted_iota(jnp.int32, (n1, DFT_N2, LANES), 1)
    ang = ((k1 * n2) % n).astype(F32) * (2.0 * math.pi / n)
    tw_c, tw_s = jnp.cos(ang), jnp.sin(ang)

    kt = DFT_K1_TILE
    y = pl.pallas_call(
        _dft_stage1_kernel,
        grid=(b, DFT_N2 // kt),
        in_specs=[
            pl.BlockSpec((2 * n1, n1), lambda bi, j: (0, 0)),
            pl.BlockSpec((None, n1, kt, c), lambda bi, j: (bi, 0, j, 0)),
        ],
        out_specs=pl.BlockSpec((None, 2, n1, kt, c), lambda bi, j: (bi, 0, 0, j, 0)),
        out_shape=jax.ShapeDtypeStruct((b, 2, n1, DFT_N2, c), F32),
        compiler_params=_cparams("arbitrary", "arbitrary"),
        name="dft_stage1",
    )(f1, f_all.reshape(b, ntot // DFT_N2, DFT_N2, c))

    const = lambda bi, j: (0, 0)
    tw_spec = pl.BlockSpec((kt, DFT_N2, LANES), lambda bi, j: (j, 0, 0))
    out = pl.pallas_call(
        _dft_stage2_kernel,
        grid=(b, n1 // kt),
        in_specs=[
            pl.BlockSpec((None, None, kt, DFT_N2, c), lambda bi, j: (bi, 0, j, 0, 0)),
            pl.BlockSpec((None, None, kt, DFT_N2, c), lambda bi, j: (bi, 1, j, 0, 0)),
            tw_spec, tw_spec,
            pl.BlockSpec(f2.shape, const),
            pl.BlockSpec(cb.shape, const),
            pl.BlockSpec(sb.shape, const),
            pl.BlockSpec(w_f.shape, const),
            pl.BlockSpec((1, c), const),
        ],
        out_specs=pl.BlockSpec((None, DFT_N2, kt, c), lambda bi, j: (bi, 0, j, 0)),
        out_shape=jax.ShapeDtypeStruct((b, DFT_N2, n1, c), F32),
        compiler_params=_cparams("arbitrary", "arbitrary"),
        name="dft_stage2",
    )(y, y, tw_c, tw_s, f2, cb, sb, w_f, b_f)
    return out.reshape(b, n, c)


def _fourier_ctx(f_all, ctx_tile, w_f, b_f):
    b, _, c = f_all.shape
    n = CTX_LEN
    scale = 1.0 / math.sqrt(n * FOURIER_HEAD_DIM)
    cn, sn = _dft_mats(n)
    fn = jnp.asarray(np.concatenate([cn, -sn], axis=0) * scale, F32)
    cb, sb = (jnp.asarray(m, F32) for m in _channel_dft_mats())
    const = lambda bi: (0, 0)
    return pl.pallas_call(
        _dft_ctx_kernel,
        grid=(b,),
        in_specs=[
            pl.BlockSpec((None, n, c), lambda bi: (bi, ctx_tile, 0)),
            pl.BlockSpec(fn.shape, const),
            pl.BlockSpec(cb.shape, const),
            pl.BlockSpec(sb.shape, const),
            pl.BlockSpec(w_f.shape, const),
            pl.BlockSpec((1, c), const),
        ],
        out_specs=pl.BlockSpec((None, n, c), lambda bi: (bi, 0, 0)),
        out_shape=jax.ShapeDtypeStruct((b, n, c), F32),
        compiler_params=_cparams("arbitrary"),
        name="dft_ctx",
    )(f_all, fn, cb, sb, w_f, b_f)


def _out_proj_kernel(x_ref, four_ref, att_ref, wf_ref, wa_ref, mod_ref, g_ref, x1_ref, h2_ref):
    mix = (jnp.dot(four_ref[...].astype(BF16), wf_ref[...], preferred_element_type=F32)
           + jnp.dot(att_ref[...], wa_ref[...], preferred_element_type=F32))
    x1 = x_ref[...] + mod_ref[2:3, :] * mix
    x1_ref[...] = x1
    gain = g_ref[...] * (1.0 + mod_ref[4:5, :])
    h2_ref[...] = (_rms(x1) * gain + mod_ref[3:4, :]).astype(BF16)


def _out_proj(xx, four, att, w_out_f, w_out_a, mod_l, g_ffn, n_lat, tiles):
    b, _, d = xx.shape
    ctx_tile = n_lat // ROW_TILE
    const = lambda bi, t: (0, 0)
    row = lambda bi, t: (bi, t, 0)
    rows = tiles * ROW_TILE
    return pl.pallas_call(
        _out_proj_kernel,
        grid=(b, tiles),
        in_specs=[
            pl.BlockSpec((None, ROW_TILE, d), row),
            pl.BlockSpec((None, ROW_TILE, FOURIER_WIDTH), row),
            pl.BlockSpec((None, ROW_TILE, MLA_WIDTH), row),
            pl.BlockSpec(w_out_f.shape, const, pipeline_mode=pl.Buffered(1)),
            pl.BlockSpec(w_out_a.shape, const, pipeline_mode=pl.Buffered(1)),
            pl.BlockSpec((None, N_MOD, d), lambda bi, t: (jnp.where(t == ctx_tile, b, bi), 0, 0)),
            pl.BlockSpec((1, d), const),
        ],
        out_specs=[pl.BlockSpec((None, ROW_TILE, d), row), pl.BlockSpec((None, ROW_TILE, d), row)],
        out_shape=[jax.ShapeDtypeStruct((b, rows, d), F32), jax.ShapeDtypeStruct((b, rows, d), BF16)],
        compiler_params=_cparams("arbitrary", "arbitrary"),
        name="out_proj",
    )(xx, four, att, w_out_f, w_out_a, mod_l, g_ffn)


def _conv_ffn_kernel(h_ref, hp_ref, hn_ref, x_ref, wup_ref, wdw_ref, bdw_ref, wdn_ref, mod_ref, gf_ref,
                     o_ref, *, n_lat, final):
    t = pl.program_id(1)
    T = ROW_TILE
    lat_tiles = n_lat // T
    is_ctx = t == lat_tiles
    has_prev = jnp.logical_and(t != 0, t != lat_tiles)
    has_next = t < lat_tiles - 1
    h = h_ref[...]
    hrow = lax.broadcasted_iota(jnp.int32, (BF16_SUBLANES, 1), 0)
    halo = (jnp.where(jnp.logical_and(hrow == BF16_SUBLANES - 1, has_prev), hp_ref[...], 0)
            + jnp.where(jnp.logical_and(hrow == 0, has_next), hn_ref[...], 0)).astype(BF16)
    r8 = lax.broadcasted_iota(jnp.int32, (8, 1), 0)
    ctx_end = CTX_LEN

    def up(col):
        w = wup_ref[:, col:col + FF_CHUNK]
        return jnp.dot(h, w, preferred_element_type=F32), jnp.dot(halo, w, preferred_element_type=F32)

    def conv(col, u, uh):
        u_prev = uh[BF16_SUBLANES - 1:, :]
        u_next = uh[:1, :]
        below = pltpu.roll(u, 1, axis=0)
        above = pltpu.roll(u, T - 1, axis=0)
        below = jnp.concatenate([jnp.where(r8 == 0, u_prev, below[:8]), below[8:]], axis=0)
        last = jnp.where(r8 == 7, u_next, above[T - 8:])
        mid = jnp.where(jnp.logical_and(r8 == 7, is_ctx), 0.0, above[ctx_end - 8:ctx_end])
        above = jnp.concatenate([above[:ctx_end - 8], mid, above[ctx_end:T - 8], last], axis=0)
        wd = wdw_ref[:, col:col + FF_CHUNK]
        return below * wd[0:1] + u * wd[1:2] + above * wd[2:3] + bdw_ref[:, col:col + FF_CHUNK]

    def vec(ci, raw):
        (ug, uhg), (uv, uhv) = raw
        gate = conv(ci * FF_CHUNK, ug, uhg)
        val = conv(D_FF + ci * FF_CHUNK, uv, uhv)
        return (gate / (1.0 + jnp.exp(-gate)) * val).astype(BF16)

    def upc(ci):
        return up(ci * FF_CHUNK), up(D_FF + ci * FF_CHUNK)

    n_chunks = D_FF // FF_CHUNK
    acc = None
    raw = upc(0)
    act_prev = None
    for ci in range(n_chunks):
        raw_next = upc(ci + 1) if ci + 1 < n_chunks else None
        if act_prev is not None:
            d = jnp.dot(act_prev, wdn_ref[(ci - 1) * FF_CHUNK:ci * FF_CHUNK, :], preferred_element_type=F32)
            acc = d if acc is None else acc + d
        act_prev = vec(ci, raw)
        raw = raw_next
    acc = acc + jnp.dot(act_prev, wdn_ref[(n_chunks - 1) * FF_CHUNK:, :], preferred_element_type=F32)
    out = x_ref[...] + mod_ref[5:6, :] * acc
    if final:
        out = _rms(out) * gf_ref[...]
    o_ref[...] = out


def _conv_ffn(x1, h2, w_up, w_dw, b_dw, w_down, mod_l, g_final, n_lat, final):
    b, rows, d = x1.shape
    T = ROW_TILE
    tiles = rows // T
    ctx_tile = n_lat // T
    halo = BF16_SUBLANES
    per_tile = T // halo
    last_halo = rows // halo - 1
    const = lambda bi, t: (0, 0)
    row = lambda bi, t: (bi, t, 0)
    wspec = lambda shape: pl.BlockSpec(shape, const, pipeline_mode=pl.Buffered(1))
    return pl.pallas_call(
        functools.partial(_conv_ffn_kernel, n_lat=n_lat, final=final),
        grid=(b, tiles),
        in_specs=[
            pl.BlockSpec((None, T, d), row),
            pl.BlockSpec((None, halo, d), lambda bi, t: (bi, jnp.maximum(t * per_tile - 1, 0), 0)),
            pl.BlockSpec((None, halo, d), lambda bi, t: (bi, jnp.minimum((t + 1) * per_tile, last_halo), 0)),
            pl.BlockSpec((None, T, d), row),
            wspec(w_up.shape),
            pl.BlockSpec(w_dw.shape, const),
            pl.BlockSpec((1, 2 * D_FF), const),
            wspec(w_down.shape),
            pl.BlockSpec((None, N_MOD, d), lambda bi, t: (jnp.where(t == ctx_tile, b, bi), 0, 0)),
            pl.BlockSpec((1, d), const),
        ],
        out_specs=pl.BlockSpec((None, T, d), row),
        out_shape=jax.ShapeDtypeStruct((b, rows, d), F32),
        compiler_params=_cparams("arbitrary", "arbitrary"),
        name="conv_ffn",
    )(h2, h2, h2, x1, w_up, w_dw, b_dw, w_down, mod_l, g_final)


def _prep_w_in(w_in):
    d = w_in.shape[0]
    o_kr = FOURIER_WIDTH + Q_LORA_RANK + KV_LORA_RANK
    half = QK_ROPE_DIM // 2
    x1 = w_in[:, o_kr:o_kr + half]
    x2 = w_in[:, o_kr + half:o_kr + 2 * half]
    z_lo = jnp.zeros((d, QK_NOPE_DIM), w_in.dtype)
    z_hi = jnp.zeros((d, HEAD_PAD - QK_HEAD_DIM), w_in.dtype)
    main = jnp.concatenate([z_lo, x1, x2, z_hi], axis=1)
    swap = jnp.concatenate([z_lo, -x2, x1, z_hi], axis=1)
    return jnp.concatenate([w_in[:, :o_kr], main, swap], axis=1).astype(BF16)


def _prep_w_q(w_q_b):
    r = w_q_b.shape[0]
    half = QK_ROPE_DIM // 2
    w = w_q_b.reshape(r, N_HEADS, QK_HEAD_DIM)
    nope = w[..., :QK_NOPE_DIM]
    x1 = w[..., QK_NOPE_DIM:QK_NOPE_DIM + half]
    x2 = w[..., QK_NOPE_DIM + half:]
    z_hi = jnp.zeros((r, N_HEADS, HEAD_PAD - QK_HEAD_DIM), w.dtype)
    main = jnp.concatenate([nope, x1, x2, z_hi], axis=-1).reshape(r, N_HEADS * HEAD_PAD)
    swap = jnp.concatenate([jnp.zeros_like(nope), -x2, x1, z_hi], axis=-1).reshape(r, N_HEADS * HEAD_PAD)
    return jnp.concatenate([main, swap], axis=1).astype(BF16)


def _prep_w_kv(w_kv_b):
    r = w_kv_b.shape[0]
    w = w_kv_b.reshape(r, N_HEADS, QK_NOPE_DIM + V_HEAD_DIM)
    z = jnp.zeros((r, N_HEADS, HEAD_PAD - QK_NOPE_DIM), w.dtype)
    k = jnp.concatenate([w[..., :QK_NOPE_DIM], z], axis=-1).reshape(r, N_HEADS * HEAD_PAD)
    zv = jnp.zeros((r, N_HEADS, VT_ROWS - V_HEAD_DIM), w.dtype)
    v = jnp.concatenate([w[..., QK_NOPE_DIM:], zv], axis=-1).reshape(r, N_HEADS * VT_ROWS)
    return k.astype(BF16), v.T.astype(BF16)


def _rope_tables(n_lat, n_rest):
    rows = n_lat // GRID_W
    row_ids = jnp.broadcast_to(jnp.arange(rows)[:, None], (rows, GRID_W)).reshape(-1).astype(F32)
    col_ids = jnp.broadcast_to(jnp.arange(GRID_W)[None, :], (rows, GRID_W)).reshape(-1).astype(F32)
    n_freq = QK_ROPE_DIM // 4
    inv_freq = ROPE_THETA ** (-jnp.arange(n_freq, dtype=F32) / n_freq)
    ang = jnp.concatenate([row_ids[:, None] * inv_freq, col_ids[:, None] * inv_freq], axis=-1)
    cos, sin = jnp.cos(ang), jnp.sin(ang)
    cos = jnp.concatenate([cos, jnp.ones((n_rest, cos.shape[1]), F32)], axis=0)
    sin = jnp.concatenate([sin, jnp.zeros((n_rest, sin.shape[1]), F32)], axis=0)
    nt = n_lat + n_rest
    ones_lo = jnp.ones((nt, QK_NOPE_DIM), F32)
    ones_hi = jnp.ones((nt, HEAD_PAD - QK_HEAD_DIM), F32)
    cos_t = jnp.concatenate([ones_lo, cos, cos, ones_hi], axis=1)
    sin_t = jnp.concatenate([0.0 * ones_lo, sin, sin, 0.0 * ones_hi], axis=1)
    qs = SOFTMAX_SCALE * LOG2E
    return cos_t * qs, sin_t * qs, cos_t, sin_t


def kernel(x, c, ctx, c_ctx, w_ada, b_ada, g_mix, w_in, w_fourier, b_fourier, g_q_a, w_q_b, g_kv_a, w_kv_b,
           w_out, g_ffn, w_up, w_dw, b_dw, w_down, g_final):
    b, n_lat, d = x.shape
    n_ctx = ctx.shape[1]
    depth = w_ada.shape[0]
    assert n_ctx == CTX_LEN == Q_TILE and 2 * n_ctx == ROW_TILE == KV_CHUNK
    assert n_lat % ROW_TILE == 0 and n_lat % (DFT_N2 * DFT_K1_TILE) == 0
    lat_tiles = n_lat // ROW_TILE
    n_pad = ROW_TILE - n_ctx

    pad = (-(b + 1)) % 8
    cvec = jnp.concatenate([c, c_ctx[None, :], jnp.zeros((pad, d), F32)], axis=0)
    mod = _modulation(cvec, w_ada, b_ada).reshape(depth, b + 1 + pad, N_MOD, d)

    tabs = _rope_tables(n_lat, n_ctx + n_pad)
    xx = jnp.concatenate([x, ctx, jnp.zeros((b, n_pad, d), F32)], axis=1)

    for l in range(depth):
        last = l == depth - 1
        tiles = lat_tiles if last else lat_tiles + 1
        w_k_p, w_vt_p = _prep_w_kv(w_kv_b[l])
        f_in, q, k, vt = _in_proj(xx, mod[l], g_mix[l][None], _prep_w_in(w_in[l]),
                                  g_q_a[l][None], _prep_w_q(w_q_b[l]),
                                  g_kv_a[l][None], w_k_p, w_vt_p, tabs, n_lat)
        att = _attention(q, k, vt, n_lat, tiles * (ROW_TILE // Q_TILE))
        w_f = w_fourier[l].astype(BF16)
        b_f = b_fourier[l][None]
        four = _fourier_latent(f_in, n_lat, w_f, b_f)
        if not last:
            four_c = _fourier_ctx(f_in, n_lat // CTX_LEN, w_f, b_f)
            four = jnp.concatenate([four, four_c, jnp.zeros((b, n_pad, FOURIER_WIDTH), F32)], axis=1)
        w_o = w_out[l].astype(BF16)
        x1, h2 = _out_proj(xx, four, att, w_o[:FOURIER_WIDTH], w_o[FOURIER_WIDTH:], mod[l], g_ffn[l][None],
                           n_lat, tiles)
        xx = _conv_ffn(x1, h2, w_up[l].astype(BF16), w_dw[l], b_dw[l][None], w_down[l].astype(BF16),
                       mod[l], g_final[None], n_lat, last)
    return xx
```

```python
import functools
import math

import numpy as np
import jax
import jax.numpy as jnp
from jax import lax
from jax.experimental import pallas as pl
from jax.experimental.pallas import tpu as pltpu

F32 = jnp.float32
BF16 = jnp.bfloat16

D_MODEL = 1024
CTX_LEN = 256
GRID_W = 64
FOURIER_WIDTH = 256
N_FOURIER_HEADS = 4
FOURIER_HEAD_DIM = 64
V_HEAD_DIM = 64
QK_NOPE_DIM = 64
QK_ROPE_DIM = 32
QK_HEAD_DIM = 96
N_HEADS = 12
MLA_WIDTH = 768
Q_LORA_RANK = 384
KV_LORA_RANK = 128
D_FF = 2816
ROPE_THETA = 10000.0
NORM_EPS = 1e-6
SOFTMAX_SCALE = QK_HEAD_DIM ** -0.5
N_MOD = 6

LANES = 128
BF16_SUBLANES = 16
HEAD_PAD = LANES
ROPE_HALF = QK_ROPE_DIM // 2
ROW_TILE = 512
Q_TILE = 256
HEADS_PER_STEP = 4
KV_CHUNK = 512
VT_ROWS = V_HEAD_DIM + BF16_SUBLANES
M_INIT = -1e30
FF_CHUNK = 256
DFT_N2 = 64
DFT_K1_TILE = 8
VMEM_LIMIT = 56 * 1024 * 1024
LOG2E = math.log2(math.e)
HIGHEST = lax.Precision.HIGHEST


def _cparams(*sem):
    return pltpu.CompilerParams(dimension_semantics=sem, vmem_limit_bytes=VMEM_LIMIT)


def _rms(x, eps=NORM_EPS):
    return x * lax.rsqrt(jnp.mean(x * x, axis=-1, keepdims=True) + eps)


def _mod_kernel(c_ref, w_ref, b_ref, o_ref):
    c = c_ref[...]
    s = c / (1.0 + jnp.exp(-c))
    o_ref[...] = jnp.dot(s.astype(BF16), w_ref[...].astype(BF16),
                         preferred_element_type=F32) + b_ref[...]


def _modulation(cvec, w_ada, b_ada):
    depth, d, _ = w_ada.shape
    rows = cvec.shape[0]
    return pl.pallas_call(
        _mod_kernel,
        grid=(depth, N_MOD),
        in_specs=[
            pl.BlockSpec((rows, d), lambda l, j: (0, 0)),
            pl.BlockSpec((None, d, d), lambda l, j: (l, 0, j)),
            pl.BlockSpec((None, 1, d), lambda l, j: (l, 0, j)),
        ],
        out_specs=pl.BlockSpec((None, rows, d), lambda l, j: (l, 0, j)),
        out_shape=jax.ShapeDtypeStruct((depth, rows, N_MOD * d), F32),
        compiler_params=_cparams("arbitrary", "arbitrary"),
        name="adaln_mod",
    )(cvec, w_ada, b_ada.reshape(depth, 1, N_MOD * d))


def _in_proj_kernel(x_ref, mod_ref, g_ref, w_in_ref, gq_ref, wq_ref, gkv_ref, wk_ref, wvt_ref,
                    cq_ref, saq_ref, sbq_ref, ck_ref, sak_ref, sbk_ref,
                    f_ref, q_ref, k_ref, vt_ref, *, ctx_tile):
    x = x_ref[...]
    gain = g_ref[...] * (1.0 + mod_ref[1:2, :])
    h = _rms(x) * gain + mod_ref[0:1, :]
    p = jnp.dot(h.astype(BF16), w_in_ref[...], preferred_element_type=F32)
    f_ref[...] = p[:, :FOURIER_WIDTH]

    o_q = FOURIER_WIDTH
    o_kv = o_q + Q_LORA_RANK
    o_kr = o_kv + KV_LORA_RANK
    hw = N_HEADS * HEAD_PAD

    cq = _rms(p[:, o_q:o_kv]) * gq_ref[...]
    qq = jnp.dot(cq.astype(BF16), wq_ref[...], preferred_element_type=F32)

    def rope(blk, cos, sa, sb):
        return (blk * cos + pltpu.roll(blk, HEAD_PAD - ROPE_HALF, axis=1) * sa
                + pltpu.roll(blk, ROPE_HALF, axis=1) * sb)

    cos_q, sa_q, sb_q = cq_ref[...], saq_ref[...], sbq_ref[...]
    for h_i in range(N_HEADS):
        lo = h_i * HEAD_PAD
        q_ref[h_i] = rope(qq[:, lo:lo + HEAD_PAD], cos_q, sa_q, sb_q).astype(BF16)

    ckv = (_rms(p[:, o_kv:o_kr]) * gkv_ref[...]).astype(BF16)
    kn = jnp.dot(ckv, wk_ref[...], preferred_element_type=F32)
    k_rope = rope(p[:, o_kr:o_kr + HEAD_PAD], ck_ref[...], sak_ref[...], sbk_ref[...])
    for h_i in range(N_HEADS):
        lo = h_i * HEAD_PAD
        k_ref[h_i] = (kn[:, lo:lo + HEAD_PAD] + k_rope).astype(BF16)
    vt = lax.dot_general(wvt_ref[...], ckv, (((1,), (1,)), ((), ())), preferred_element_type=F32)
    ones_row = (lax.broadcasted_iota(jnp.int32, (VT_ROWS, 1), 0) == V_HEAD_DIM).astype(F32)
    for h_i in range(N_HEADS):
        vt_ref[h_i] = (vt[h_i * VT_ROWS:(h_i + 1) * VT_ROWS] + ones_row).astype(BF16)

    @pl.when(pl.program_id(1) == ctx_tile)
    def _():
        k_ref[:, CTX_LEN:, :] = k_ref[:, :CTX_LEN, :]
        vt_ref[:, :, CTX_LEN:] = jnp.zeros((N_HEADS, VT_ROWS, ROW_TILE - CTX_LEN), BF16)


def _in_proj(xx, mod_l, g_mix, w_in_p, g_q, w_q_p, g_kv, w_k_p, w_vt_p, tabs, n_lat):
    b, nt, d = xx.shape
    tiles = nt // ROW_TILE
    ctx_tile = n_lat // ROW_TILE
    const = lambda bi, t: (0, 0)
    wspec = lambda w: pl.BlockSpec(w.shape, const, pipeline_mode=pl.Buffered(1))
    tab_spec = pl.BlockSpec((ROW_TILE, HEAD_PAD), lambda bi, t: (t, 0))
    head_spec = pl.BlockSpec((None, N_HEADS, ROW_TILE, HEAD_PAD), lambda bi, t: (bi, 0, t, 0))
    head_shape = jax.ShapeDtypeStruct((b, N_HEADS, nt, HEAD_PAD), BF16)
    return pl.pallas_call(
        functools.partial(_in_proj_kernel, ctx_tile=ctx_tile),
        grid=(b, tiles),
        in_specs=[
            pl.BlockSpec((None, ROW_TILE, d), lambda bi, t: (bi, t, 0)),
            pl.BlockSpec((None, N_MOD, d), lambda bi, t: (jnp.where(t == ctx_tile, b, bi), 0, 0)),
            pl.BlockSpec((1, d), const),
            wspec(w_in_p),
            pl.BlockSpec((1, Q_LORA_RANK), const),
            wspec(w_q_p),
            pl.BlockSpec((1, KV_LORA_RANK), const),
            wspec(w_k_p),
            wspec(w_vt_p),
            *([tab_spec] * len(tabs)),
        ],
        out_specs=[
            pl.BlockSpec((None, ROW_TILE, FOURIER_WIDTH), lambda bi, t: (bi, t, 0)),
            head_spec, head_spec,
            pl.BlockSpec((None, N_HEADS, VT_ROWS, ROW_TILE), lambda bi, t: (bi, 0, 0, t)),
        ],
        out_shape=[
            jax.ShapeDtypeStruct((b, nt, FOURIER_WIDTH), F32),
            head_shape, head_shape,
            jax.ShapeDtypeStruct((b, N_HEADS, VT_ROWS, nt), BF16),
        ],
        compiler_params=_cparams("arbitrary", "arbitrary"),
        name="in_proj",
    )(xx, mod_l, g_mix, w_in_p, g_q, w_q_p, g_kv, w_k_p, w_vt_p, *tabs)


def _attention_kernel(q_ref, k_ref, vt_ref, o_ref, s_scr, p_scr, m_scr, a_scr, acc_scr, *, n_lat):
    t = pl.program_id(2)
    n_all = k_ref.shape[1] // KV_CHUNK
    lat_chunks = n_lat // KV_CHUNK
    nt_dims = (((1,), (1,)), ((), ()))

    def qk(hh, start):
        return lax.dot_general(k_ref[hh, pl.ds(start, KV_CHUNK), :], q_ref[hh], nt_dims,
                               preferred_element_type=F32)

    def softmax(hh, s):
        m = m_scr[hh]
        m_new = jnp.maximum(m, jnp.max(s, axis=0, keepdims=True))
        m_scr[hh] = m_new
        a_scr[hh] = jnp.exp2(m - m_new)
        return jnp.exp2(s - m_new).astype(BF16)

    def pv(hh, p, start):
        acc_scr[hh] = a_scr[hh] * acc_scr[hh] + jnp.dot(vt_ref[hh, :, pl.ds(start, KV_CHUNK)], p,
                                                        preferred_element_type=F32)

    is_lat = t < n_lat // Q_TILE
    first = pl.multiple_of(jnp.where(is_lat, 0, lat_chunks) * KV_CHUNK, KV_CHUNK)
    for hh in range(HEADS_PER_STEP):
        m_scr[hh] = jnp.full((1, Q_TILE), M_INIT, F32)
        a_scr[hh] = jnp.ones((1, Q_TILE), F32)
        acc_scr[hh] = jnp.zeros((VT_ROWS, Q_TILE), F32)
        s_scr[hh] = qk(hh, first)

    @pl.when(is_lat)
    def _():
        for c in range(n_all - 1):
            s_cur = [s_scr[hh] for hh in range(HEADS_PER_STEP)]
            for hh in range(HEADS_PER_STEP):
                s_scr[hh] = qk(hh, (c + 1) * KV_CHUNK)
                if c > 0:
                    pv(hh, p_scr[hh], (c - 1) * KV_CHUNK)
            for hh in range(HEADS_PER_STEP):
                p_scr[hh] = softmax(hh, s_cur[hh])
        for hh in range(HEADS_PER_STEP):
            pv(hh, p_scr[hh], (n_all - 2) * KV_CHUNK)

    outs = []
    for hh in range(HEADS_PER_STEP):
        pv(hh, softmax(hh, s_scr[hh]), (n_all - 1) * KV_CHUNK)
        acc = acc_scr[hh]
        outs.append(acc[:V_HEAD_DIM] / acc[V_HEAD_DIM:V_HEAD_DIM + 1])
    o_ref[...] = jnp.concatenate(outs, axis=0).T.astype(o_ref.dtype)


def _attention(q, k, vt, n_lat, q_tiles):
    b, _, nkp, _ = k.shape
    return pl.pallas_call(
        functools.partial(_attention_kernel, n_lat=n_lat),
        grid=(b, N_HEADS // HEADS_PER_STEP, q_tiles),
        in_specs=[
            pl.BlockSpec((None, HEADS_PER_STEP, Q_TILE, HEAD_PAD), lambda bi, hp, t: (bi, hp, t, 0)),
            pl.BlockSpec((None, HEADS_PER_STEP, nkp, HEAD_PAD), lambda bi, hp, t: (bi, hp, 0, 0)),
            pl.BlockSpec((None, HEADS_PER_STEP, VT_ROWS, nkp), lambda bi, hp, t: (bi, hp, 0, 0)),
        ],
        out_specs=pl.BlockSpec((None, Q_TILE, HEADS_PER_STEP * V_HEAD_DIM), lambda bi, hp, t: (bi, t, hp)),
        out_shape=jax.ShapeDtypeStruct((b, q_tiles * Q_TILE, MLA_WIDTH), BF16),
        scratch_shapes=[
            pltpu.VMEM((HEADS_PER_STEP, KV_CHUNK, Q_TILE), F32),
            pltpu.VMEM((HEADS_PER_STEP, KV_CHUNK, Q_TILE), BF16),
            pltpu.VMEM((HEADS_PER_STEP, 1, Q_TILE), F32),
            pltpu.VMEM((HEADS_PER_STEP, 1, Q_TILE), F32),
            pltpu.VMEM((HEADS_PER_STEP, VT_ROWS, Q_TILE), F32),
        ],
        compiler_params=_cparams("arbitrary", "arbitrary", "arbitrary"),
        name="mla_attention",
    )(q, k, vt)


def _dft_mats(n):
    idx = np.arange(n)
    ang = 2.0 * np.pi * ((idx[:, None] * idx[None, :]) % n) / n
    return np.cos(ang), np.sin(ang)


def _channel_dft_mats():
    c, s = _dft_mats(FOURIER_HEAD_DIM)
    eye = np.eye(N_FOURIER_HEADS)
    return np.kron(eye, c), np.kron(eye, s)


def _dft_stage1_kernel(f1_ref, x_ref, y_ref):
    n1 = x_ref.shape[0]
    for jj in range(x_ref.shape[1]):
        y = jnp.dot(f1_ref[...], x_ref[:, jj, :], precision=HIGHEST, preferred_element_type=F32)
        y_ref[0, :, jj, :] = y[:n1]
        y_ref[1, :, jj, :] = y[n1:]


def _dft_stage2_kernel(yr_ref, yi_ref, tc_ref, ts_ref, f2_ref, cb_ref, sb_ref, wf_ref, bf_ref, o_ref):
    xr, xi = [], []
    for j in range(DFT_K1_TILE):
        yr, yi = yr_ref[j], yi_ref[j]
        tc = jnp.concatenate([tc_ref[j]] * (FOURIER_WIDTH // LANES), axis=-1)
        ts = jnp.concatenate([ts_ref[j]] * (FOURIER_WIDTH // LANES), axis=-1)
        z = jnp.concatenate([yr * tc + yi * ts, yi * tc - yr * ts], axis=0)
        xx = jnp.dot(f2_ref[...], z, precision=HIGHEST, preferred_element_type=F32)
        xr.append(xx[:DFT_N2])
        xi.append(xx[DFT_N2:])
    xr = jnp.concatenate(xr, axis=0)
    xi = jnp.concatenate(xi, axis=0)
    four = _channel_mix(xr, xi, cb_ref, sb_ref, wf_ref, bf_ref)
    for j in range(DFT_K1_TILE):
        o_ref[:, j, :] = four[j * DFT_N2:(j + 1) * DFT_N2]


def _channel_mix(xr, xi, cb_ref, sb_ref, wf_ref, bf_ref):
    f = (jnp.dot(xr, cb_ref[...], precision=HIGHEST, preferred_element_type=F32)
         + jnp.dot(xi, sb_ref[...], precision=HIGHEST, preferred_element_type=F32))
    return jnp.dot(f.astype(BF16), wf_ref[...], preferred_element_type=F32) + bf_ref[...]


def _dft_ctx_kernel(u_ref, fn_ref, cb_ref, sb_ref, wf_ref, bf_ref, o_ref):
    n = u_ref.shape[0]
    xx = jnp.dot(fn_ref[...], u_ref[...], precision=HIGHEST, preferred_element_type=F32)
    o_ref[...] = _channel_mix(xx[:n], xx[n:], cb_ref, sb_ref, wf_ref, bf_ref)


def _fourier_latent(f_all, n, w_f, b_f):
    b, ntot, c = f_all.shape
    n1 = n // DFT_N2
    scale = 1.0 / math.sqrt(n * FOURIER_HEAD_DIM)
    c1, s1 = _dft_mats(n1)
    f1 = jnp.asarray(np.concatenate([c1, -s1], axis=0) * scale, F32)
    c2, s2 = _dft_mats(DFT_N2)
    f2 = jnp.asarray(np.block([[c2, s2], [-s2, c2]]), F32)
    cb, sb = (jnp.asarray(m, F32) for m in _channel_dft_mats())
    k1 = lax.broadcasted_iota(jnp.int32, (n1, DFT_N2, LANES), 0)
    n2 = lax.broadcasted_iota(jnp.int32, (n1, DFT_N2, LANES), 1)
    ang = ((k1 * n2) % n).astype(F32) * (2.0 * math.pi / n)
    tw_c, tw_s = jnp.cos(ang), jnp.sin(ang)

    kt = DFT_K1_TILE
    y = pl.pallas_call(
        _dft_stage1_kernel,
        grid=(b, DFT_N2 // kt),
        in_specs=[
            pl.BlockSpec((2 * n1, n1), lambda bi, j: (0, 0)),
            pl.BlockSpec((None, n1, kt, c), lambda bi, j: (bi, 0, j, 0)),
        ],
        out_specs=pl.BlockSpec((None, 2, n1, kt, c), lambda bi, j: (bi, 0, 0, j, 0)),
        out_shape=jax.ShapeDtypeStruct((b, 2, n1, DFT_N2, c), F32),
        compiler_params=_cparams("arbitrary", "arbitrary"),
        name="dft_stage1",
    )(f1, f_all.reshape(b, ntot // DFT_N2, DFT_N2, c))

    const = lambda bi, j: (0, 0)
    tw_spec = pl.BlockSpec((kt, DFT_N2, LANES), lambda bi, j: (j, 0, 0))
    out = pl.pallas_call(
        _dft_stage2_kernel,
        grid=(b, n1 // kt),
        in_specs=[
            pl.BlockSpec((None, None, kt, DFT_N2, c), lambda bi, j: (bi, 0, j, 0, 0)),
            pl.BlockSpec((None, None, kt, DFT_N2, c), lambda bi, j: (bi, 1, j, 0, 0)),
            tw_spec, tw_spec,
            pl.BlockSpec(f2.shape, const),
            pl.BlockSpec(cb.shape, const),
            pl.BlockSpec(sb.shape, const),
            pl.BlockSpec(w_f.shape, const),
            pl.BlockSpec((1, c), const),
        ],
        out_specs=pl.BlockSpec((None, DFT_N2, kt, c), lambda bi, j: (bi, 0, j, 0)),
        out_shape=jax.ShapeDtypeStruct((b, DFT_N2, n1, c), F32),
        compiler_params=_cparams("arbitrary", "arbitrary"),
        name="dft_stage2",
    )(y, y, tw_c, tw_s, f2, cb, sb, w_f, b_f)
    return out.reshape(b, n, c)


def _fourier_ctx(f_all, ctx_tile, w_f, b_f):
    b, _, c = f_all.shape
    n = CTX_LEN
    scale = 1.0 / math.sqrt(n * FOURIER_HEAD_DIM)
    cn, sn = _dft_mats(n)
    fn = jnp.asarray(np.concatenate([cn, -sn], axis=0) * scale, F32)
    cb, sb = (jnp.asarray(m, F32) for m in _channel_dft_mats())
    const = lambda bi: (0, 0)
    return pl.pallas_call(
        _dft_ctx_kernel,
        grid=(b,),
        in_specs=[
            pl.BlockSpec((None, n, c), lambda bi: (bi, ctx_tile, 0)),
            pl.BlockSpec(fn.shape, const),
            pl.BlockSpec(cb.shape, const),
            pl.BlockSpec(sb.shape, const),
            pl.BlockSpec(w_f.shape, const),
            pl.BlockSpec((1, c), const),
        ],
        out_specs=pl.BlockSpec((None, n, c), lambda bi: (bi, 0, 0)),
        out_shape=jax.ShapeDtypeStruct((b, n, c), F32),
        compiler_params=_cparams("arbitrary"),
        name="dft_ctx",
    )(f_all, fn, cb, sb, w_f, b_f)


def _out_proj_kernel(x_ref, four_ref, att_ref, wf_ref, wa_ref, mod_ref, g_ref, x1_ref, h2_ref):
    mix = (jnp.dot(four_ref[...].astype(BF16), wf_ref[...], preferred_element_type=F32)
           + jnp.dot(att_ref[...], wa_ref[...], preferred_element_type=F32))
    x1 = x_ref[...] + mod_ref[2:3, :] * mix
    x1_ref[...] = x1
    gain = g_ref[...] * (1.0 + mod_ref[4:5, :])
    h2_ref[...] = (_rms(x1) * gain + mod_ref[3:4, :]).astype(BF16)


def _out_proj(xx, four, att, w_out_f, w_out_a, mod_l, g_ffn, n_lat, tiles):
    b, _, d = xx.shape
    ctx_tile = n_lat // ROW_TILE
    const = lambda bi, t: (0, 0)
    row = lambda bi, t: (bi, t, 0)
    rows = tiles * ROW_TILE
    return pl.pallas_call(
        _out_proj_kernel,
        grid=(b, tiles),
        in_specs=[
            pl.BlockSpec((None, ROW_TILE, d), row),
            pl.BlockSpec((None, ROW_TILE, FOURIER_WIDTH), row),
            pl.BlockSpec((None, ROW_TILE, MLA_WIDTH), row),
            pl.BlockSpec(w_out_f.shape, const, pipeline_mode=pl.Buffered(1)),
            pl.BlockSpec(w_out_a.shape, const, pipeline_mode=pl.Buffered(1)),
            pl.BlockSpec((None, N_MOD, d), lambda bi, t: (jnp.where(t == ctx_tile, b, bi), 0, 0)),
            pl.BlockSpec((1, d), const),
        ],
        out_specs=[pl.BlockSpec((None, ROW_TILE, d), row), pl.BlockSpec((None, ROW_TILE, d), row)],
        out_shape=[jax.ShapeDtypeStruct((b, rows, d), F32), jax.ShapeDtypeStruct((b, rows, d), BF16)],
        compiler_params=_cparams("arbitrary", "arbitrary"),
        name="out_proj",
    )(xx, four, att, w_out_f, w_out_a, mod_l, g_ffn)


def _conv_ffn_kernel(h_ref, hp_ref, hn_ref, x_ref, wup_ref, wdw_ref, bdw_ref, wdn_ref, mod_ref, gf_ref,
                     o_ref, *, n_lat, final):
    t = pl.program_id(1)
    T = ROW_TILE
    lat_tiles = n_lat // T
    is_ctx = t == lat_tiles
    has_prev = jnp.logical_and(t != 0, t != lat_tiles)
    has_next = t < lat_tiles - 1
    h = h_ref[...]
    hrow = lax.broadcasted_iota(jnp.int32, (BF16_SUBLANES, 1), 0)
    halo = (jnp.where(jnp.logical_and(hrow == BF16_SUBLANES - 1, has_prev), hp_ref[...], 0)
            + jnp.where(jnp.logical_and(hrow == 0, has_next), hn_ref[...], 0)).astype(BF16)
    r8 = lax.broadcasted_iota(jnp.int32, (8, 1), 0)
    ctx_end = CTX_LEN

    def up(col):
        w = wup_ref[:, col:col + FF_CHUNK]
        return jnp.dot(h, w, preferred_element_type=F32), jnp.dot(halo, w, preferred_element_type=F32)

    def conv(col, u, uh):
        u_prev = uh[BF16_SUBLANES - 1:, :]
        u_next = uh[:1, :]
        below = pltpu.roll(u, 1, axis=0)
        above = pltpu.roll(u, T - 1, axis=0)
        below = jnp.concatenate([jnp.where(r8 == 0, u_prev, below[:8]), below[8:]], axis=0)
        last = jnp.where(r8 == 7, u_next, above[T - 8:])
        mid = jnp.where(jnp.logical_and(r8 == 7, is_ctx), 0.0, above[ctx_end - 8:ctx_end])
        above = jnp.concatenate([above[:ctx_end - 8], mid, above[ctx_end:T - 8], last], axis=0)
        wd = wdw_ref[:, col:col + FF_CHUNK]
        return below * wd[0:1] + u * wd[1:2] + above * wd[2:3] + bdw_ref[:, col:col + FF_CHUNK]

    def vec(ci, raw):
        (ug, uhg), (uv, uhv) = raw
        gate = conv(ci * FF_CHUNK, ug, uhg)
        val = conv(D_FF + ci * FF_CHUNK, uv, uhv)
        return (gate / (1.0 + jnp.exp(-gate)) * val).astype(BF16)

    def upc(ci):
        return up(ci * FF_CHUNK), up(D_FF + ci * FF_CHUNK)

    n_chunks = D_FF // FF_CHUNK
    acc = None
    raw = upc(0)
    act_prev = None
    for ci in range(n_chunks):
        raw_next = upc(ci + 1) if ci + 1 < n_chunks else None
        if act_prev is not None:
            d = jnp.dot(act_prev, wdn_ref[(ci - 1) * FF_CHUNK:ci * FF_CHUNK, :], preferred_element_type=F32)
            acc = d if acc is None else acc + d
        act_prev = vec(ci, raw)
        raw = raw_next
    acc = acc + jnp.dot(act_prev, wdn_ref[(n_chunks - 1) * FF_CHUNK:, :], preferred_element_type=F32)
    out = x_ref[...] + mod_ref[5:6, :] * acc
    if final:
        out = _rms(out) * gf_ref[...]
    o_ref[...] = out


def _conv_ffn(x1, h2, w_up, w_dw, b_dw, w_down, mod_l, g_final, n_lat, final):
    b, rows, d = x1.shape
    T = ROW_TILE
    tiles = rows // T
    ctx_tile = n_lat // T
    halo = BF16_SUBLANES
    per_tile = T // halo
    last_halo = rows // halo - 1
    const = lambda bi, t: (0, 0)
    row = lambda bi, t: (bi, t, 0)
    wspec = lambda shape: pl.BlockSpec(shape, const, pipeline_mode=pl.Buffered(1))
    return pl.pallas_call(
        functools.partial(_conv_ffn_kernel, n_lat=n_lat, final=final),
        grid=(b, tiles),
        in_specs=[
            pl.BlockSpec((None, T, d), row),
            pl.BlockSpec((None, halo, d), lambda bi, t: (bi, jnp.maximum(t * per_tile - 1, 0), 0)),
            pl.BlockSpec((None, halo, d), lambda bi, t: (bi, jnp.minimum((t + 1) * per_tile, last_halo), 0)),
            pl.BlockSpec((None, T, d), row),
            wspec(w_up.shape),
            pl.BlockSpec(w_dw.shape, const),
            pl.BlockSpec((1, 2 * D_FF), const),
            wspec(w_down.shape),
            pl.BlockSpec((None, N_MOD, d), lambda bi, t: (jnp.where(t == ctx_tile, b, bi), 0, 0)),
            pl.BlockSpec((1, d), const),
        ],
        out_specs=pl.BlockSpec((None, T, d), row),
        out_shape=jax.ShapeDtypeStruct((b, rows, d), F32),
        compiler_params=_cparams("arbitrary", "arbitrary"),
        name="conv_ffn",
    )(h2, h2, h2, x1, w_up, w_dw, b_dw, w_down, mod_l, g_final)


def _prep_w_in(w_in):
    d = w_in.shape[0]
    o_kr = FOURIER_WIDTH + Q_LORA_RANK + KV_LORA_RANK
    z_lo = jnp.zeros((d, QK_NOPE_DIM), w_in.dtype)
    z_hi = jnp.zeros((d, HEAD_PAD - QK_HEAD_DIM), w_in.dtype)
    return jnp.concatenate([w_in[:, :o_kr], z_lo, w_in[:, o_kr:], z_hi], axis=1).astype(BF16)


def _prep_w_q(w_q_b):
    r = w_q_b.shape[0]
    w = w_q_b.reshape(r, N_HEADS, QK_HEAD_DIM)
    z_hi = jnp.zeros((r, N_HEADS, HEAD_PAD - QK_HEAD_DIM), w.dtype)
    return jnp.concatenate([w, z_hi], axis=-1).reshape(r, N_HEADS * HEAD_PAD).astype(BF16)


def _prep_w_kv(w_kv_b):
    r = w_kv_b.shape[0]
    w = w_kv_b.reshape(r, N_HEADS, QK_NOPE_DIM + V_HEAD_DIM)
    z = jnp.zeros((r, N_HEADS, HEAD_PAD - QK_NOPE_DIM), w.dtype)
    k = jnp.concatenate([w[..., :QK_NOPE_DIM], z], axis=-1).reshape(r, N_HEADS * HEAD_PAD)
    zv = jnp.zeros((r, N_HEADS, VT_ROWS - V_HEAD_DIM), w.dtype)
    v = jnp.concatenate([w[..., QK_NOPE_DIM:], zv], axis=-1).reshape(r, N_HEADS * VT_ROWS)
    return k.astype(BF16), v.T.astype(BF16)


def _rope_tables(n_lat, n_rest):
    rows = n_lat // GRID_W
    row_ids = jnp.broadcast_to(jnp.arange(rows)[:, None], (rows, GRID_W)).reshape(-1).astype(F32)
    col_ids = jnp.broadcast_to(jnp.arange(GRID_W)[None, :], (rows, GRID_W)).reshape(-1).astype(F32)
    n_freq = QK_ROPE_DIM // 4
    inv_freq = ROPE_THETA ** (-jnp.arange(n_freq, dtype=F32) / n_freq)
    ang = jnp.concatenate([row_ids[:, None] * inv_freq, col_ids[:, None] * inv_freq], axis=-1)
    cos, sin = jnp.cos(ang), jnp.sin(ang)
    cos = jnp.concatenate([cos, jnp.ones((n_rest, cos.shape[1]), F32)], axis=0)
    sin = jnp.concatenate([sin, jnp.zeros((n_rest, sin.shape[1]), F32)], axis=0)
    nt = n_lat + n_rest
    ones_lo = jnp.ones((nt, QK_NOPE_DIM), F32)
    ones_hi = jnp.ones((nt, HEAD_PAD - QK_HEAD_DIM), F32)
    zero = jnp.zeros_like(sin)
    cos_t = jnp.concatenate([ones_lo, cos, cos, ones_hi], axis=1)
    sa_t = jnp.concatenate([0.0 * ones_lo, -sin, zero, 0.0 * ones_hi], axis=1)
    sb_t = jnp.concatenate([0.0 * ones_lo, zero, sin, 0.0 * ones_hi], axis=1)
    qs = SOFTMAX_SCALE * LOG2E
    return cos_t * qs, sa_t * qs, sb_t * qs, cos_t, sa_t, sb_t


def kernel(x, c, ctx, c_ctx, w_ada, b_ada, g_mix, w_in, w_fourier, b_fourier, g_q_a, w_q_b, g_kv_a, w_kv_b,
           w_out, g_ffn, w_up, w_dw, b_dw, w_down, g_final):
    b, n_lat, d = x.shape
    n_ctx = ctx.shape[1]
    depth = w_ada.shape[0]
    assert n_ctx == CTX_LEN == Q_TILE and 2 * n_ctx == ROW_TILE == KV_CHUNK
    assert n_lat % ROW_TILE == 0 and n_lat % (DFT_N2 * DFT_K1_TILE) == 0
    lat_tiles = n_lat // ROW_TILE
    n_pad = ROW_TILE - n_ctx

    pad = (-(b + 1)) % 8
    cvec = jnp.concatenate([c, c_ctx[None, :], jnp.zeros((pad, d), F32)], axis=0)
    mod = _modulation(cvec, w_ada, b_ada).reshape(depth, b + 1 + pad, N_MOD, d)

    tabs = _rope_tables(n_lat, n_ctx + n_pad)
    xx = jnp.concatenate([x, ctx, jnp.zeros((b, n_pad, d), F32)], axis=1)

    for l in range(depth):
        last = l == depth - 1
        tiles = lat_tiles if last else lat_tiles + 1
        w_k_p, w_vt_p = _prep_w_kv(w_kv_b[l])
        f_in, q, k, vt = _in_proj(xx, mod[l], g_mix[l][None], _prep_w_in(w_in[l]),
                                  g_q_a[l][None], _prep_w_q(w_q_b[l]),
                                  g_kv_a[l][None], w_k_p, w_vt_p, tabs, n_lat)
        att = _attention(q, k, vt, n_lat, tiles * (ROW_TILE // Q_TILE))
        w_f = w_fourier[l].astype(BF16)
        b_f = b_fourier[l][None]
        four = _fourier_latent(f_in, n_lat, w_f, b_f)
        if not last:
            four_c = _fourier_ctx(f_in, n_lat // CTX_LEN, w_f, b_f)
            four = jnp.concatenate([four, four_c, jnp.zeros((b, n_pad, FOURIER_WIDTH), F32)], axis=1)
        w_o = w_out[l].astype(BF16)
        x1, h2 = _out_proj(xx, four, att, w_o[:FOURIER_WIDTH], w_o[FOURIER_WIDTH:], mod[l], g_ffn[l][None],
                           n_lat, tiles)
        xx = _conv_ffn(x1, h2, w_up[l].astype(BF16), w_dw[l], b_dw[l][None], w_down[l].astype(BF16),
                       mod[l], g_final[None], n_lat, last)
    return xx
```
